```python
import math
import jax
import jax.numpy as jnp
from jax import lax
import numpy as np

D_MODEL = 1024
BATCH = 8
SEQ = 4096
DEPTH = 2

N_MIXERS = 2
MEM_LEN = 256
EPS = 1e-6

A_HEADS = 12
A_KV_HEADS = 2
A_HEAD_DIM = 64
WINDOW = 128
BLOCK = 128
N_BUCKETS = 32
MAX_DISTANCE = 128

B_QK_HEADS = 3
B_V_HEADS = 6
B_HEAD_DIM = 128
B_CONV = 4
CHUNK = 64

X_HEADS = 4
X_HEAD_DIM = 64

D_FF = 2816
FFN_CONV = 3

A_Q = A_HEADS * A_HEAD_DIM
A_KV = A_KV_HEADS * A_HEAD_DIM
X_Q = X_HEADS * X_HEAD_DIM
B_QK = B_QK_HEADS * B_HEAD_DIM
B_V = B_V_HEADS * B_HEAD_DIM
B_QKV = 2 * B_QK + B_V
IN_A = A_Q + 2 * A_KV + X_Q
IN_B = B_QKV + B_V + 2 * B_V_HEADS + X_Q
MIX_WIDTH = A_Q + X_Q
N_A_LAYERS = (DEPTH + 1) // 2
N_B_LAYERS = DEPTH // 2

kernel_name = "hybrid_swa_sink_deltanet_memxattn_convglu"


def rms_norm(x, g):
    xf = x.astype(jnp.float32)
    y = xf * lax.rsqrt(jnp.mean(xf * xf, axis=-1, keepdims=True) + EPS)
    return (y * g.astype(jnp.float32)).astype(x.dtype)


def l2_normalize(x):
    xf = x.astype(jnp.float32)
    return xf * lax.rsqrt(jnp.sum(xf * xf, axis=-1, keepdims=True) + EPS)


def causal_depthwise_conv(x, w):
    k = w.shape[0]
    return lax.conv_general_dilated(
        x, w[:, None, :].astype(x.dtype), window_strides=(1,), padding=[(k - 1, 0)],
        dimension_numbers=('NWC', 'WIO', 'NWC'), feature_group_count=x.shape[-1])


def t5_causal_bucket(dist):
    n = jnp.maximum(dist, 0)
    max_exact = N_BUCKETS // 2
    nf = jnp.maximum(n, 1).astype(jnp.float32)
    large = max_exact + (jnp.log(nf / max_exact) / math.log(MAX_DISTANCE / max_exact)
                         * (N_BUCKETS - max_exact)).astype(jnp.int32)
    large = jnp.minimum(large, N_BUCKETS - 1)
    return jnp.where(n < max_exact, n, large)


def swa_sink_attention(q, k, v, sinks, rel_bias):
    b, s, _, dh = q.shape
    nb = s // BLOCK
    grp = A_HEADS // A_KV_HEADS
    qb = q.reshape(b, nb, BLOCK, A_KV_HEADS, grp, dh)

    def band(t):
        tb = t.reshape(b, nb, BLOCK, A_KV_HEADS, dh)
        prev = jnp.concatenate([jnp.zeros_like(tb[:, :1]), tb[:, :-1]], axis=1)
        return jnp.concatenate([prev, tb], axis=2)

    kb, vb = band(k), band(v)
    qi = jnp.arange(BLOCK)[:, None]
    kj = jnp.arange(2 * BLOCK)[None, :]
    dist = BLOCK + qi - kj
    band_ok = (dist >= 0) & (dist < WINDOW)
    blk_ok = (jnp.arange(nb)[:, None, None] > 0) | (kj[None] >= BLOCK)
    mask = band_ok[None] & blk_ok
    bias = rel_bias[t5_causal_bucket(dist)]
    bias = jnp.transpose(bias, (2, 0, 1)).reshape(A_KV_HEADS, grp, BLOCK, 2 * BLOCK)

    scores = jnp.einsum('bnqhgd,bnkhd->bnhgqk', qb, kb,
                        preferred_element_type=jnp.float32) * (dh ** -0.5)
    scores = scores + bias.astype(jnp.float32)
    scores = jnp.where(mask[None, :, None, None], scores, -jnp.inf)
    sink = sinks.astype(jnp.float32).reshape(1, 1, A_KV_HEADS, grp, 1, 1)
    m = jnp.maximum(jnp.max(scores, axis=-1, keepdims=True), sink)
    p = jnp.exp(scores - m)
    probs = p / (jnp.sum(p, axis=-1, keepdims=True) + jnp.exp(sink - m))
    out = jnp.einsum('bnhgqk,bnkhd->bnqhgd', probs.astype(v.dtype), vb)
    return out.reshape(b, s, A_HEADS * dh)


def memory_cross_attention(q, mem_k, mem_v):
    b, s = q.shape[0], q.shape[1]
    scores = jnp.einsum('bshd,bmhd->bhsm', q, mem_k,
                        preferred_element_type=jnp.float32) * (X_HEAD_DIM ** -0.5)
    probs = jax.nn.softmax(scores, axis=-1).astype(mem_v.dtype)
    out = jnp.einsum('bhsm,bmhd->bshd', probs, mem_v)
    return out.reshape(b, s, X_Q)


def gated_delta_rule(q, k, v, g, beta):
    b, s, h, dk = q.shape
    dv = v.shape[-1]
    nc = s // CHUNK
    f32 = jnp.float32

    def chunks(t):
        t = t.astype(f32).reshape((b, nc, CHUNK, h) + t.shape[3:])
        return jnp.moveaxis(t, 3, 1)

    qc = chunks(q) * (dk ** -0.5)
    kc = chunks(k)
    vc = chunks(v)
    bc = chunks(beta)
    gc = jnp.cumsum(chunks(g), axis=-1)
    idx = jnp.arange(CHUNK)
    strict = idx[:, None] > idx[None, :]
    incl = idx[:, None] >= idx[None, :]
    gdiff = gc[..., :, None] - gc[..., None, :]
    decay_incl = jnp.exp(jnp.where(incl, gdiff, -jnp.inf))
    kk = jnp.einsum('bhntd,bhnsd->bhnts', kc, kc)
    a_mat = bc[..., :, None] * kk * jnp.where(strict, decay_incl, 0.0)
    eye = jnp.eye(CHUNK, dtype=f32)
    rhs = jnp.concatenate([bc[..., None] * vc, (bc * jnp.exp(gc))[..., None] * kc], axis=-1)
    sol = lax.linalg.triangular_solve(eye + a_mat, rhs, left_side=True, lower=True,
                                      unit_diagonal=True)
    u, w = sol[..., :dv], sol[..., dv:]
    attn_qk = jnp.einsum('bhntd,bhnsd->bhnts', qc, kc) * decay_incl
    q_decay = qc * jnp.exp(gc)[..., None]
    k_tail = kc * jnp.exp(gc[..., -1:] - gc)[..., None]
    decay_chunk = jnp.exp(gc[..., -1])

    def step(state, xs):
        u_n, w_n, a_n, qd_n, kt_n, dc_n = xs
        delta = u_n - jnp.einsum('bhtk,bhkv->bhtv', w_n, state)
        out = (jnp.einsum('bhtk,bhkv->bhtv', qd_n, state)
               + jnp.einsum('bhts,bhsv->bhtv', a_n, delta))
        state = dc_n[..., None, None] * state + jnp.einsum('bhsk,bhsv->bhkv', kt_n, delta)
        return state, out

    xs = tuple(jnp.moveaxis(t, 2, 0) for t in (u, w, attn_qk, q_decay, k_tail, decay_chunk))
    state0 = jnp.zeros((b, h, dk, dv), f32)
    _, out = lax.scan(step, state0, xs)
    return jnp.transpose(out, (1, 0, 3, 2, 4)).reshape(b, s, h, dv)


def swa_group(proj, sinks, rel_bias):
    b, s, _ = proj.shape
    q, k, v = jnp.split(proj, [A_Q, A_Q + A_KV], axis=-1)
    q = q.reshape(b, s, A_HEADS, A_HEAD_DIM)
    k = k.reshape(b, s, A_KV_HEADS, A_HEAD_DIM)
    v = v.reshape(b, s, A_KV_HEADS, A_HEAD_DIM)
    return swa_sink_attention(q, k, v, sinks, rel_bias)


def deltanet_group(proj, conv_w, a_log, dt_bias, norm_g):
    b, s, _ = proj.shape
    f32 = jnp.float32
    qkv, z, beta_logit, a_logit = jnp.split(
        proj, [B_QKV, B_QKV + B_V, B_QKV + B_V + B_V_HEADS], axis=-1)
    qkv = jax.nn.silu(causal_depthwise_conv(qkv, conv_w))
    q, k, v = jnp.split(qkv, [B_QK, 2 * B_QK], axis=-1)
    rep = B_V_HEADS // B_QK_HEADS
    q = jnp.repeat(l2_normalize(q.reshape(b, s, B_QK_HEADS, B_HEAD_DIM)), rep, axis=2)
    k = jnp.repeat(l2_normalize(k.reshape(b, s, B_QK_HEADS, B_HEAD_DIM)), rep, axis=2)
    v = v.reshape(b, s, B_V_HEADS, B_HEAD_DIM)
    beta = jax.nn.sigmoid(beta_logit.astype(f32))
    g = -jnp.exp(a_log.astype(f32)) * jax.nn.softplus(a_logit.astype(f32) + dt_bias.astype(f32))
    o = gated_delta_rule(q, k, v, g, beta)
    o = o * lax.rsqrt(jnp.mean(o * o, axis=-1, keepdims=True) + EPS) * norm_g.astype(f32)
    o = o * jax.nn.silu(z.reshape(b, s, B_V_HEADS, B_HEAD_DIM).astype(f32))
    return o.reshape(b, s, B_V).astype(proj.dtype)


def conv_glu(hn, w_gate_up, conv_w, conv_b, w_down):
    gate, up = jnp.split(hn @ w_gate_up, [D_FF], axis=-1)
    gate = causal_depthwise_conv(gate, conv_w) + conv_b
    return (jax.nn.silu(gate) * up) @ w_down


def setup_inputs(seed: int = 0) -> dict:
    key = jax.random.key(seed)
    ks = jax.random.split(key, 24)
    f32 = jnp.float32

    def dense(k, shape, fan_in):
        return jax.random.normal(k, shape, f32) * (fan_in ** -0.5)

    def gain(k, shape):
        return 1.0 + 0.02 * jax.random.normal(k, shape, f32)

    dt = jnp.exp(jax.random.uniform(ks[13], (N_B_LAYERS, B_V_HEADS), f32,
                                    math.log(1e-3), math.log(1e-1)))
    return {
        'x': jax.random.normal(ks[0], (BATCH, SEQ, D_MODEL), f32),
        'mem': jax.random.normal(ks[1], (BATCH, MEM_LEN, D_MODEL), f32),
        'rel_bias': 0.5 * jax.random.normal(ks[2], (N_BUCKETS, A_HEADS), f32),
        'norm_mix_g': gain(ks[3], (DEPTH, D_MODEL)),
        'norm_mem_g': gain(ks[4], (DEPTH, D_MODEL)),
        'w_mem_kv': dense(ks[5], (DEPTH, D_MODEL, 2 * X_Q), D_MODEL),
        'w_out': dense(ks[6], (DEPTH, MIX_WIDTH, D_MODEL), MIX_WIDTH),
        'w_in_a': dense(ks[7], (N_A_LAYERS, D_MODEL, IN_A), D_MODEL),
        'sinks_a': 0.5 * jax.random.normal(ks[8], (N_A_LAYERS, A_HEADS), f32),
        'w_in_b': dense(ks[9], (N_B_LAYERS, D_MODEL, IN_B), D_MODEL),
        'conv_qkv_b': dense(ks[10], (N_B_LAYERS, B_CONV, B_QKV), B_CONV),
        'a_log_b': jnp.log(jax.random.uniform(ks[11], (N_B_LAYERS, B_V_HEADS), f32, 1.0, 16.0)),
        'dt_bias_b': dt + jnp.log(-jnp.expm1(-dt)),
        'out_norm_g_b': gain(ks[12], (N_B_LAYERS, B_HEAD_DIM)),
        'norm_ffn_g': gain(ks[14], (DEPTH, D_MODEL)),
        'w_gate_up': dense(ks[15], (DEPTH, D_MODEL, 2 * D_FF), D_MODEL),
        'ffn_conv_w': dense(ks[16], (DEPTH, FFN_CONV, D_FF), FFN_CONV),
        'ffn_conv_b': 0.02 * jax.random.normal(ks[17], (DEPTH, D_FF), f32),
        'w_down': dense(ks[18], (DEPTH, D_FF, D_MODEL), D_FF),
        'final_norm_g': gain(ks[19], (D_MODEL,)),
    }


def reference(x, mem, rel_bias, norm_mix_g, norm_mem_g, w_mem_kv, w_out, w_in_a, sinks_a,
              w_in_b, conv_qkv_b, a_log_b, dt_bias_b, out_norm_g_b, norm_ffn_g, w_gate_up,
              ffn_conv_w, ffn_conv_b, w_down, final_norm_g):
    b, s, _ = x.shape
    h = x
    for i in range(DEPTH):
        j = i // N_MIXERS
        hn = rms_norm(h, norm_mix_g[i])
        mem_kv = rms_norm(mem, norm_mem_g[i]) @ w_mem_kv[i]
        mem_k = mem_kv[..., :X_Q].reshape(b, MEM_LEN, X_HEADS, X_HEAD_DIM)
        mem_v = mem_kv[..., X_Q:].reshape(b, MEM_LEN, X_HEADS, X_HEAD_DIM)
        if i % N_MIXERS == 0:
            proj = hn @ w_in_a[j]
            self_out = swa_group(proj[..., :IN_A - X_Q], sinks_a[j], rel_bias)
        else:
            proj = hn @ w_in_b[j]
            self_out = deltanet_group(proj[..., :IN_B - X_Q], conv_qkv_b[j], a_log_b[j],
                                      dt_bias_b[j], out_norm_g_b[j])
        xq = proj[..., -X_Q:].reshape(b, s, X_HEADS, X_HEAD_DIM)
        cross_out = memory_cross_attention(xq, mem_k, mem_v)
        h = h + jnp.concatenate([self_out, cross_out], axis=-1) @ w_out[i]
        h = h + conv_glu(rms_norm(h, norm_ffn_g[i]), w_gate_up[i], ffn_conv_w[i],
                         ffn_conv_b[i], w_down[i])
    return rms_norm(h, final_norm_g)
```

```python
import functools
import math

import numpy as np
import jax
import jax.numpy as jnp
from jax import lax
from jax.experimental import pallas as pl
from jax.experimental.pallas import tpu as pltpu

D_MODEL = 1024
MEM_LEN = 256
EPS = 1e-6

A_HEADS = 12
A_KV_HEADS = 2
A_HEAD_DIM = 64
A_GROUP = A_HEADS // A_KV_HEADS
WINDOW = 128
BLOCK = 128
N_BUCKETS = 32
MAX_DISTANCE = 128

B_QK_HEADS = 3
B_V_HEADS = 6
B_HEAD_DIM = 128
B_CONV = 4
CHUNK = 64
SUB = 16

X_HEADS = 4
X_HEAD_DIM = 64

D_FF = 2816
FFN_CONV = 3

A_Q = A_HEADS * A_HEAD_DIM
A_KV = A_KV_HEADS * A_HEAD_DIM
X_Q = X_HEADS * X_HEAD_DIM
B_QK = B_QK_HEADS * B_HEAD_DIM
B_V = B_V_HEADS * B_HEAD_DIM
B_QKV = 2 * B_QK + B_V
IN_A = A_Q + 2 * A_KV + X_Q
IN_B = B_QKV + B_V + 2 * B_V_HEADS + X_Q

LANES = 128
SUBLANES = 8
IN_B_PAD = B_QKV + B_V + X_Q + 2 * LANES

SEQ_TILE = 512
FF_CHUNK = 256
VMEM_LIMIT = 56 * 1024 * 1024

F32 = jnp.float32
BF16 = jnp.bfloat16
HI = lax.Precision.HIGHEST
NEG_INF = float("-inf")


def _dot(a, b, precision=None):
    return jnp.dot(a, b, preferred_element_type=F32, precision=precision)


def _dot_nt(a, b, precision=None):
    return lax.dot_general(a, b, (((1,), (1,)), ((), ())), preferred_element_type=F32,
                           precision=precision)


def _dot_tn(a, b, precision=None):
    return lax.dot_general(a, b, (((0,), (0,)), ((), ())), preferred_element_type=F32,
                           precision=precision)


def _rms(x, g):
    return x * lax.rsqrt(jnp.mean(x * x, axis=-1, keepdims=True) + EPS) * g


def _silu(x):
    return x / (1.0 + jnp.exp(-x))


def _lane_halves(shape):
    return lax.broadcasted_iota(jnp.int32, shape, len(shape) - 1) < (LANES // 2)


def _mem_kv_kernel(mem_ref, g_ref, w_ref, k_ref, v_ref):
    mn = _rms(mem_ref[0], g_ref[0]).astype(BF16)
    kv = _dot(mn, w_ref[0])
    low = _lane_halves((MEM_LEN, LANES))
    for p in range(X_HEADS // 2):
        kp = kv[:, p * LANES:(p + 1) * LANES]
        vp = kv[:, X_Q + p * LANES:X_Q + (p + 1) * LANES]
        k_ref[0, 0, p, 0:MEM_LEN, :] = jnp.where(low, kp, 0.0).astype(BF16)
        k_ref[0, 0, p, MEM_LEN:2 * MEM_LEN, :] = jnp.where(low, 0.0, kp).astype(BF16)
        v_ref[0, 0, p, 0:MEM_LEN, :] = jnp.where(low, vp, 0.0).astype(BF16)
        v_ref[0, 0, p, MEM_LEN:2 * MEM_LEN, :] = jnp.where(low, 0.0, vp).astype(BF16)


def _mem_kv(mem, norm_mem_g, w_mem_kv):
    depth, batch = w_mem_kv.shape[0], mem.shape[0]
    out = jax.ShapeDtypeStruct((depth, batch, X_HEADS // 2, 2 * MEM_LEN, LANES), BF16)
    out_spec = pl.BlockSpec((1, 1, X_HEADS // 2, 2 * MEM_LEN, LANES), lambda l, b: (l, b, 0, 0, 0))
    return pl.pallas_call(
        _mem_kv_kernel,
        grid=(depth, batch),
        in_specs=[
            pl.BlockSpec((1, MEM_LEN, D_MODEL), lambda l, b: (b, 0, 0)),
            pl.BlockSpec((1, 1, D_MODEL), lambda l, b: (l, 0, 0)),
            pl.BlockSpec((1, D_MODEL, 2 * X_Q), lambda l, b: (l, 0, 0)),
        ],
        out_specs=[out_spec, out_spec],
        out_shape=[out, out],
        compiler_params=pltpu.CompilerParams(dimension_semantics=("arbitrary", "arbitrary")),
    )(mem, norm_mem_g.reshape(depth, 1, D_MODEL), w_mem_kv.astype(BF16))


def _cross_attention(xq_ref, mkk_ref, mvv_ref, cat_ref, col0):
    ts = xq_ref.shape[0]
    low = _lane_halves((ts, LANES))
    for p in range(X_HEADS // 2):
        xp = xq_ref[:, p * LANES:(p + 1) * LANES]
        outs, recips = [], []
        for half in range(2):
            rows = slice(half * MEM_LEN, (half + 1) * MEM_LEN)
            sc = _dot_nt(xp, mkk_ref[0, 0, p, rows, :])
            e = jnp.exp(sc - jnp.max(sc, axis=-1, keepdims=True))
            recips.append(1.0 / jnp.sum(e, axis=-1, keepdims=True))
            outs.append(_dot(e.astype(BF16), mvv_ref[0, 0, p, rows, :]))
        o = (outs[0] + outs[1]) * jnp.where(low, recips[0], recips[1])
        cat_ref[:, col0 + p * LANES:col0 + (p + 1) * LANES] = o.astype(BF16)


def _t5_bucket_table():
    qi = np.arange(BLOCK)[:, None]
    kj = np.arange(2 * BLOCK)[None, :]
    dist = BLOCK + qi - kj
    n = np.maximum(dist, 0)
    max_exact = N_BUCKETS // 2
    nf = np.maximum(n, 1).astype(np.float32)
    large = max_exact + (np.log(nf / np.float32(max_exact)) / np.float32(math.log(MAX_DISTANCE / max_exact))
                         * np.float32(N_BUCKETS - max_exact)).astype(np.int32)
    large = np.minimum(large, N_BUCKETS - 1)
    bucket = np.where(n < max_exact, n, large)
    ok = (dist >= 0) & (dist < WINDOW)
    return np.where(ok, bucket, -1).astype(np.int32)


def _mixer_a_kernel(rel_ref, sink_ref, x_ref, g_ref, win_ref, wout_ref, mkk_ref, mvv_ref,
                    bucket_ref, o_ref, bias_ref, q_ref, ka_ref, kb_ref, va_ref, vb_ref, xq_ref,
                    cat_ref):
    ts = x_ref.shape[1]
    b_idx, s_idx = pl.program_id(0), pl.program_id(1)

    @pl.when((b_idx == 0) & (s_idx == 0))
    def _build_bias():
        bucket = bucket_ref[...]
        prev_half = lax.broadcasted_iota(jnp.int32, (BLOCK, 2 * BLOCK), 1) < BLOCK

        def head(h, carry):
            acc = jnp.full((BLOCK, 2 * BLOCK), NEG_INF, F32)
            for bk in range(N_BUCKETS):
                acc = jnp.where(bucket == bk, rel_ref[bk, h], acc)
            bias_ref[0, h] = acc
            bias_ref[1, h] = jnp.where(prev_half, NEG_INF, acc)
            return carry

        lax.fori_loop(0, A_HEADS, head, 0)

    @pl.when(s_idx == 0)
    def _reset_carry():
        zeros = jnp.zeros((BLOCK, LANES), BF16)
        ka_ref[0:BLOCK, :] = zeros
        kb_ref[0:BLOCK, :] = zeros
        va_ref[0:BLOCK, :] = zeros
        vb_ref[0:BLOCK, :] = zeros

    x = x_ref[0]
    xn = _rms(x, g_ref[...]).astype(BF16)
    proj = _dot(xn, win_ref[...])
    q_ref[...] = proj[:, 0:A_Q].astype(BF16)
    k = proj[:, A_Q:A_Q + A_KV]
    v = proj[:, A_Q + A_KV:A_Q + 2 * A_KV]
    xq_ref[...] = proj[:, A_Q + 2 * A_KV:IN_A].astype(BF16)
    low = _lane_halves((ts, LANES))
    ka_ref[BLOCK:BLOCK + ts, :] = jnp.where(low, k, 0.0).astype(BF16)
    kb_ref[BLOCK:BLOCK + ts, :] = jnp.where(low, 0.0, k).astype(BF16)
    va_ref[BLOCK:BLOCK + ts, :] = jnp.where(low, v, 0.0).astype(BF16)
    vb_ref[BLOCK:BLOCK + ts, :] = jnp.where(low, 0.0, v).astype(BF16)

    first = jnp.where(s_idx == 0, 1, 0)
    low_blk = _lane_halves((BLOCK, LANES))
    for i in range(ts // BLOCK):
        band = slice(i * BLOCK, (i + 2) * BLOCK)
        ka, kb, va, vb = ka_ref[band, :], kb_ref[band, :], va_ref[band, :], vb_ref[band, :]
        sel = first if i == 0 else 0
        for j in range(A_GROUP):
            qp = q_ref[i * BLOCK:(i + 1) * BLOCK, j * LANES:(j + 1) * LANES]
            outs, recips = [], []
            for h, kk, vv in ((j, ka, va), (A_GROUP + j, kb, vb)):
                sink = sink_ref[0, h]
                sc = _dot_nt(qp, kk) + bias_ref[sel, h]
                m = jnp.maximum(jnp.max(sc, axis=-1, keepdims=True), sink)
                p = jnp.exp(sc - m)
                recips.append(1.0 / (jnp.sum(p, axis=-1, keepdims=True) + jnp.exp(sink - m)))
                outs.append(_dot(p.astype(BF16), vv))
            o = (outs[0] + outs[1]) * jnp.where(low_blk, recips[0], recips[1])
            cat_ref[i * BLOCK:(i + 1) * BLOCK, j * LANES:(j + 1) * LANES] = o.astype(BF16)

    ka_ref[0:BLOCK, :] = ka_ref[ts:ts + BLOCK, :]
    kb_ref[0:BLOCK, :] = kb_ref[ts:ts + BLOCK, :]
    va_ref[0:BLOCK, :] = va_ref[ts:ts + BLOCK, :]
    vb_ref[0:BLOCK, :] = vb_ref[ts:ts + BLOCK, :]

    _cross_attention(xq_ref, mkk_ref, mvv_ref, cat_ref, A_Q)
    o_ref[0] = x + _dot(cat_ref[...], wout_ref[...])


def _const_spec(shape):
    n = len(shape)
    return pl.BlockSpec(shape, lambda b, s: (0,) * n)


def _mixer_a(h, layer, rel_bias, sinks, norm_g, w_in, w_out, mkk, mvv):
    batch, seq, _ = h.shape
    ts = SEQ_TILE
    q_cols = np.concatenate([
        np.concatenate([np.arange(j * A_HEAD_DIM, (j + 1) * A_HEAD_DIM),
                        np.arange((A_GROUP + j) * A_HEAD_DIM, (A_GROUP + j + 1) * A_HEAD_DIM)])
        for j in range(A_GROUP)])
    w_q = w_in[:, q_cols] * (A_HEAD_DIM ** -0.5)
    w_xq = w_in[:, IN_A - X_Q:] * (X_HEAD_DIM ** -0.5)
    w_in_k = jnp.concatenate([w_q, w_in[:, A_Q:A_Q + 2 * A_KV], w_xq], axis=1).astype(BF16)
    out_rows = np.concatenate([q_cols, np.arange(A_Q, A_Q + X_Q)])
    w_out_k = w_out[out_rows, :].astype(BF16)
    bucket = jnp.asarray(_t5_bucket_table())
    mem_spec = pl.BlockSpec((1, 1, X_HEADS // 2, 2 * MEM_LEN, LANES), lambda b, s: (layer, b, 0, 0, 0))
    return pl.pallas_call(
        _mixer_a_kernel,
        grid=(batch, seq // ts),
        in_specs=[
            pl.BlockSpec(memory_space=pltpu.SMEM),
            pl.BlockSpec(memory_space=pltpu.SMEM),
            pl.BlockSpec((1, ts, D_MODEL), lambda b, s: (b, s, 0)),
            _const_spec((1, D_MODEL)),
            _const_spec((D_MODEL, IN_A)),
            _const_spec((A_Q + X_Q, D_MODEL)),
            mem_spec, mem_spec,
            _const_spec((BLOCK, 2 * BLOCK)),
        ],
        out_specs=pl.BlockSpec((1, ts, D_MODEL), lambda b, s: (b, s, 0)),
        out_shape=jax.ShapeDtypeStruct(h.shape, F32),
        scratch_shapes=[
            pltpu.VMEM((2, A_HEADS, BLOCK, 2 * BLOCK), F32),
            pltpu.VMEM((ts, A_Q), BF16),
            pltpu.VMEM((ts + BLOCK, LANES), BF16),
            pltpu.VMEM((ts + BLOCK, LANES), BF16),
            pltpu.VMEM((ts + BLOCK, LANES), BF16),
            pltpu.VMEM((ts + BLOCK, LANES), BF16),
            pltpu.VMEM((ts, X_Q), BF16),
            pltpu.VMEM((ts, A_Q + X_Q), BF16),
        ],
        compiler_params=pltpu.CompilerParams(dimension_semantics=("arbitrary", "arbitrary"),
                                             vmem_limit_bytes=VMEM_LIMIT),
    )(rel_bias, sinks.reshape(1, A_HEADS), h, norm_g.reshape(1, D_MODEL), w_in_k, w_out_k,
      mkk, mvv, bucket)


def _ffn_kernel(x_ref, g_ref, wg_ref, wu_ref, cw_ref, cb_ref, wd_ref, fg_ref, o_ref,
                xn_ref, gbuf_ref, tail_ref, acc_ref, *, final_norm):
    ts = x_ref.shape[1]
    n_chunks = wg_ref.shape[0]

    @pl.when(pl.program_id(1) == 0)
    def _reset_tail():
        tail_ref[...] = jnp.zeros(tail_ref.shape, F32)

    x = x_ref[0]
    xn_ref[...] = _rms(x, g_ref[...]).astype(BF16)
    acc_ref[...] = x

    def chunk(c, carry):
        xn = xn_ref[...]
        gate = _dot(xn, wg_ref[c])
        up = _dot(xn, wu_ref[c])
        gbuf_ref[0:SUBLANES, :] = tail_ref[c]
        gbuf_ref[SUBLANES:SUBLANES + ts, :] = gate
        cw = cw_ref[c]
        y = (cw[2:3, :] * gate + cw[1:2, :] * gbuf_ref[SUBLANES - 1:SUBLANES - 1 + ts, :]
             + cw[0:1, :] * gbuf_ref[SUBLANES - 2:SUBLANES - 2 + ts, :] + cb_ref[c])
        tail_ref[c] = gbuf_ref[ts:ts + SUBLANES, :]
        act = (_silu(y) * up).astype(BF16)
        acc_ref[...] += _dot(act, wd_ref[c])
        return carry

    lax.fori_loop(0, n_chunks, chunk, 0)
    out = acc_ref[...]
    if final_norm:
        out = _rms(out, fg_ref[...])
    o_ref[0] = out


def _ffn(h, norm_g, w_gate_up, conv_w, conv_b, w_down, final_g, final_norm):
    batch, seq, _ = h.shape
    ts = SEQ_TILE
    nc = D_FF // FF_CHUNK
    wg = w_gate_up[:, :D_FF].reshape(D_MODEL, nc, FF_CHUNK).transpose(1, 0, 2).astype(BF16)
    wu = w_gate_up[:, D_FF:].reshape(D_MODEL, nc, FF_CHUNK).transpose(1, 0, 2).astype(BF16)
    wd = w_down.reshape(nc, FF_CHUNK, D_MODEL).astype(BF16)
    cw = conv_w.reshape(FFN_CONV, nc, FF_CHUNK).transpose(1, 0, 2)
    cb = conv_b.reshape(nc, 1, FF_CHUNK)
    return pl.pallas_call(
        functools.partial(_ffn_kernel, final_norm=final_norm),
        grid=(batch, seq // ts),
        in_specs=[
            pl.BlockSpec((1, ts, D_MODEL), lambda b, s: (b, s, 0)),
            _const_spec((1, D_MODEL)),
            _const_spec((nc, D_MODEL, FF_CHUNK)),
            _const_spec((nc, D_MODEL, FF_CHUNK)),
            _const_spec((nc, FFN_CONV, FF_CHUNK)),
            _const_spec((nc, 1, FF_CHUNK)),
            _const_spec((nc, FF_CHUNK, D_MODEL)),
            _const_spec((1, D_MODEL)),
        ],
        out_specs=pl.BlockSpec((1, ts, D_MODEL), lambda b, s: (b, s, 0)),
        out_shape=jax.ShapeDtypeStruct(h.shape, F32),
        scratch_shapes=[
            pltpu.VMEM((ts, D_MODEL), BF16),
            pltpu.VMEM((ts + SUBLANES, FF_CHUNK), F32),
            pltpu.VMEM((nc, SUBLANES, FF_CHUNK), F32),
            pltpu.VMEM((ts, D_MODEL), F32),
        ],
        compiler_params=pltpu.CompilerParams(dimension_semantics=("arbitrary", "arbitrary"),
                                             vmem_limit_bytes=VMEM_LIMIT),
    )(h, norm_g.reshape(1, D_MODEL), wg, wu, cw, cb, wd, final_g.reshape(1, D_MODEL))


def _unit_lower_inverse(a, eye, diag_blocks):
    d = jnp.where(diag_blocks, a, 0.0)
    e = a - d
    d2 = _dot(d, d, HI)
    d4 = _dot(d2, d2, HI)
    d8 = _dot(d4, d4, HI)
    x1 = _dot(eye - d, eye + d2, HI)
    x2 = _dot(eye + d4, eye + d8, HI)
    td = _dot(x1, x2, HI)
    n = _dot(td, e, HI)
    n2 = _dot(n, n, HI)
    y = _dot(eye - n, eye + n2, HI)
    return _dot(y, td, HI)


def _mixer_b_kernel(x_ref, g_ref, win_ref, cw_ref, alog_ref, dtb_ref, ong_ref, wout_ref,
                    mkk_ref, mvv_ref, o_ref,
                    cbuf_ref, q_ref, k_ref, v_ref, z_ref, gl_ref, beta_ref, xq_ref, osc_ref,
                    state_ref, cat_ref):
    ts = x_ref.shape[1]

    @pl.when(pl.program_id(1) == 0)
    def _reset_carry():
        cbuf_ref[ts:ts + SUBLANES, :] = jnp.zeros((SUBLANES, B_QKV), F32)
        state_ref[...] = jnp.zeros(state_ref.shape, F32)

    x = x_ref[0]
    xn = _rms(x, g_ref[...]).astype(BF16)
    proj = _dot(xn, win_ref[...])
    z_ref[...] = proj[:, B_QKV:B_QKV + B_V]
    xq_ref[...] = proj[:, B_QKV + B_V:B_QKV + B_V + X_Q].astype(BF16)
    ba0 = B_QKV + B_V + X_Q
    beta_ref[...] = 1.0 / (1.0 + jnp.exp(-proj[:, ba0:ba0 + LANES]))
    a_logit = proj[:, ba0 + LANES:ba0 + 2 * LANES] + dtb_ref[...]
    softplus = jnp.maximum(a_logit, 0.0) + jnp.log1p(jnp.exp(-jnp.abs(a_logit)))
    gl_ref[...] = -jnp.exp(alog_ref[...]) * softplus

    cbuf_ref[0:SUBLANES, :] = cbuf_ref[ts:ts + SUBLANES, :]
    cbuf_ref[SUBLANES:SUBLANES + ts, :] = proj[:, 0:B_QKV]
    cw = cw_ref[...]
    y = cw[B_CONV - 1:B_CONV, :] * cbuf_ref[SUBLANES:SUBLANES + ts, :]
    for tap in range(1, B_CONV):
        y = y + (cw[B_CONV - 1 - tap:B_CONV - tap, :]
                 * cbuf_ref[SUBLANES - tap:SUBLANES - tap + ts, :])
    qkv = _silu(y)
    for hh in range(B_QK_HEADS):
        qh = qkv[:, hh * LANES:(hh + 1) * LANES]
        kh = qkv[:, B_QK + hh * LANES:B_QK + (hh + 1) * LANES]
        q_ref[:, hh * LANES:(hh + 1) * LANES] = (
            qh * lax.rsqrt(jnp.sum(qh * qh, axis=-1, keepdims=True) + EPS) * (B_HEAD_DIM ** -0.5))
        k_ref[:, hh * LANES:(hh + 1) * LANES] = (
            kh * lax.rsqrt(jnp.sum(kh * kh, axis=-1, keepdims=True) + EPS))
    v_ref[...] = qkv[:, 2 * B_QK:B_QKV]

    row = lax.broadcasted_iota(jnp.int32, (CHUNK, CHUNK), 0)
    col = lax.broadcasted_iota(jnp.int32, (CHUNK, CHUNK), 1)
    incl = row >= col
    strict = row > col
    tri = jnp.where(incl, 1.0, 0.0)
    eye = jnp.where(row == col, 1.0, 0.0)
    diag_blocks = (row // SUB) == (col // SUB)
    pick = jnp.where(
        lax.broadcasted_iota(jnp.int32, (B_V_HEADS * CHUNK, LANES), 0) // CHUNK
        == lax.broadcasted_iota(jnp.int32, (B_V_HEADS * CHUNK, LANES), 1), 1.0, 0.0)

    def chunk(c, carry):
        rows = pl.ds(pl.multiple_of(c * CHUNK, CHUNK), CHUNK)
        gc = _dot(tri, gl_ref[rows, :], HI)
        gc_rows = _dot_nt(pick, gc, HI)
        beta = beta_ref[rows, :]
        for hh in range(B_QK_HEADS):
            q_c = q_ref[rows, hh * LANES:(hh + 1) * LANES]
            k_c = k_ref[rows, hh * LANES:(hh + 1) * LANES]
            k_bf = k_c.astype(BF16)
            kk = _dot_nt(k_bf, k_bf)
            qk = _dot_nt(q_c.astype(BF16), k_bf)
            for h in range(hh * (B_V_HEADS // B_QK_HEADS), (hh + 1) * (B_V_HEADS // B_QK_HEADS)):
                g_col = gc[:, h:h + 1]
                g_row = gc_rows[h * CHUNK:(h + 1) * CHUNK, :]
                g_last = gc[CHUNK - 1:CHUNK, h:h + 1]
                b_col = beta[:, h:h + 1]
                decay = jnp.exp(jnp.where(incl, g_col - g_row, NEG_INF))
                a = b_col * kk * jnp.where(strict, decay, 0.0)
                t_inv = _unit_lower_inverse(a, eye, diag_blocks)
                eg = jnp.exp(g_col)
                v_c = v_ref[rows, h * LANES:(h + 1) * LANES]
                rhs = jnp.concatenate([b_col * v_c, (b_col * eg) * k_c], axis=1)
                uw = _dot(t_inv, rhs, HI)
                u, w = uw[:, 0:LANES], uw[:, LANES:2 * LANES]
                attn = qk * decay
                q_decay = q_c * eg
                k_tail = k_c * jnp.exp(g_last - g_col)
                state = state_ref[h]
                state_bf = state.astype(BF16)
                delta = u - _dot(w.astype(BF16), state_bf)
                delta_bf = delta.astype(BF16)
                out = _dot(q_decay.astype(BF16), state_bf) + _dot(attn.astype(BF16), delta_bf)
                state_ref[h] = jnp.exp(g_last) * state + _dot_tn(k_tail.astype(BF16), delta_bf)
                osc_ref[rows, h * LANES:(h + 1) * LANES] = out
        return carry

    lax.fori_loop(0, ts // CHUNK, chunk, 0)

    ong = ong_ref[...]
    for h in range(B_V_HEADS):
        cols = slice(h * LANES, (h + 1) * LANES)
        o = osc_ref[:, cols]
        o = o * lax.rsqrt(jnp.mean(o * o, axis=-1, keepdims=True) + EPS) * ong
        cat_ref[:, cols] = (o * _silu(z_ref[:, cols])).astype(BF16)

    _cross_attention(xq_ref, mkk_ref, mvv_ref, cat_ref, B_V)
    o_ref[0] = x + _dot(cat_ref[...], wout_ref[...])


def _mixer_b(h, layer, norm_g, w_in, conv_w, a_log, dt_bias, out_norm_g, w_out, mkk, mvv):
    batch, seq, _ = h.shape
    ts = SEQ_TILE
    c0 = B_QKV + B_V
    pad = jnp.zeros((D_MODEL, LANES - B_V_HEADS), F32)
    w_in_k = jnp.concatenate([
        w_in[:, 0:c0],
        w_in[:, IN_B - X_Q:] * (X_HEAD_DIM ** -0.5),
        w_in[:, c0:c0 + B_V_HEADS], pad,
        w_in[:, c0 + B_V_HEADS:c0 + 2 * B_V_HEADS], pad], axis=1).astype(BF16)
    lane_pad = lambda t: jnp.pad(t.reshape(1, B_V_HEADS), ((0, 0), (0, LANES - B_V_HEADS)))
    mem_spec = pl.BlockSpec((1, 1, X_HEADS // 2, 2 * MEM_LEN, LANES), lambda b, s: (layer, b, 0, 0, 0))
    return pl.pallas_call(
        _mixer_b_kernel,
        grid=(batch, seq // ts),
        in_specs=[
            pl.BlockSpec((1, ts, D_MODEL), lambda b, s: (b, s, 0)),
            _const_spec((1, D_MODEL)),
            _const_spec((D_MODEL, IN_B_PAD)),
            _const_spec((B_CONV, B_QKV)),
            _const_spec((1, LANES)),
            _const_spec((1, LANES)),
            _const_spec((1, B_HEAD_DIM)),
            _const_spec((B_V + X_Q, D_MODEL)),
            mem_spec, mem_spec,
        ],
        out_specs=pl.BlockSpec((1, ts, D_MODEL), lambda b, s: (b, s, 0)),
        out_shape=jax.ShapeDtypeStruct(h.shape, F32),
        scratch_shapes=[
            pltpu.VMEM((ts + SUBLANES, B_QKV), F32),
            pltpu.VMEM((ts, B_QK), F32),
            pltpu.VMEM((ts, B_QK), F32),
            pltpu.VMEM((ts, B_V), F32),
            pltpu.VMEM((ts, B_V), F32),
            pltpu.VMEM((ts, LANES), F32),
            pltpu.VMEM((ts, LANES), F32),
            pltpu.VMEM((ts, X_Q), BF16),
            pltpu.VMEM((ts, B_V), F32),
            pltpu.VMEM((B_V_HEADS, B_HEAD_DIM, B_HEAD_DIM), F32),
            pltpu.VMEM((ts, B_V + X_Q), BF16),
        ],
        compiler_params=pltpu.CompilerParams(dimension_semantics=("arbitrary", "arbitrary"),
                                             vmem_limit_bytes=VMEM_LIMIT),
    )(h, norm_g.reshape(1, D_MODEL), w_in_k, conv_w, lane_pad(a_log), lane_pad(dt_bias),
      out_norm_g.reshape(1, B_HEAD_DIM), w_out.astype(BF16), mkk, mvv)


def kernel(x, mem, rel_bias, norm_mix_g, norm_mem_g, w_mem_kv, w_out, w_in_a, sinks_a, w_in_b,
           conv_qkv_b, a_log_b, dt_bias_b, out_norm_g_b, norm_ffn_g, w_gate_up, ffn_conv_w,
           ffn_conv_b, w_down, final_norm_g):
    mkk, mvv = _mem_kv(mem, norm_mem_g, w_mem_kv)
    h = _mixer_a(x, 0, rel_bias, sinks_a[0], norm_mix_g[0], w_in_a[0], w_out[0], mkk, mvv)
    h = _ffn(h, norm_ffn_g[0], w_gate_up[0], ffn_conv_w[0], ffn_conv_b[0], w_down[0],
             final_norm_g, False)
    h = _mixer_b(h, 1, norm_mix_g[1], w_in_b[0], conv_qkv_b[0], a_log_b[0], dt_bias_b[0],
                 out_norm_g_b[0], w_out[1], mkk, mvv)
    h = _ffn(h, norm_ffn_g[1], w_gate_up[1], ffn_conv_w[1], ffn_conv_b[1], w_down[1],
             final_norm_g, True)
    return h
```

```python
import functools
import math

import numpy as np
import jax
import jax.numpy as jnp
from jax import lax
from jax.experimental import pallas as pl
from jax.experimental.pallas import tpu as pltpu

D_MODEL = 1024
MEM_LEN = 256
EPS = 1e-6

A_HEADS = 12
A_KV_HEADS = 2
A_HEAD_DIM = 64
A_GROUP = A_HEADS // A_KV_HEADS
WINDOW = 128
BLOCK = 128
N_BUCKETS = 32
MAX_DISTANCE = 128

B_QK_HEADS = 3
B_V_HEADS = 6
B_HEAD_DIM = 128
B_CONV = 4
CHUNK = 64
SUB = 16

X_HEADS = 4
X_HEAD_DIM = 64

D_FF = 2816
FFN_CONV = 3

A_Q = A_HEADS * A_HEAD_DIM
A_KV = A_KV_HEADS * A_HEAD_DIM
X_Q = X_HEADS * X_HEAD_DIM
B_QK = B_QK_HEADS * B_HEAD_DIM
B_V = B_V_HEADS * B_HEAD_DIM
B_QKV = 2 * B_QK + B_V
IN_A = A_Q + 2 * A_KV + X_Q
IN_B = B_QKV + B_V + 2 * B_V_HEADS + X_Q

LANES = 128
SUBLANES = 8
IN_B_PAD = B_QKV + B_V + X_Q + 2 * LANES

A_TILE = 512
B_TILE = 512
FFN_TILE = 256
FF_CHUNK = 256
CHUNK_GROUP = 8
A_ITEMS = 24
PHASE_ITEMS = 12
PROJ_COLS = 256
VMEM_LIMIT = 56 * 1024 * 1024

F32 = jnp.float32
BF16 = jnp.bfloat16
NEG_INF = float("-inf")


def _dot(a, b, precision=None):
    return jnp.dot(a, b, preferred_element_type=F32, precision=precision)


def _dot_nt(a, b, precision=None):
    return lax.dot_general(a, b, (((1,), (1,)), ((), ())), preferred_element_type=F32,
                           precision=precision)


def _rms(x, g):
    return x * lax.rsqrt(jnp.mean(x * x, axis=-1, keepdims=True) + EPS) * g


def _silu(x):
    return x / (1.0 + jnp.exp(-x))


def _lane_halves(shape):
    return lax.broadcasted_iota(jnp.int32, shape, len(shape) - 1) < (LANES // 2)


def _mem_kv_kernel(mem_ref, g_ref, w_ref, k_ref, v_ref):
    batch = mem_ref.shape[0]
    mn = _rms(mem_ref[...].reshape(batch * MEM_LEN, D_MODEL), g_ref[0]).astype(BF16)
    kv = _dot(mn, w_ref[0])
    low = _lane_halves((MEM_LEN, LANES))
    for b in range(batch):
        rows = slice(b * MEM_LEN, (b + 1) * MEM_LEN)
        for p in range(X_HEADS // 2):
            kp = kv[rows, p * LANES:(p + 1) * LANES]
            vp = kv[rows, X_Q + p * LANES:X_Q + (p + 1) * LANES]
            k_ref[0, b, p, 0:MEM_LEN, :] = jnp.where(low, kp, 0.0).astype(BF16)
            k_ref[0, b, p, MEM_LEN:2 * MEM_LEN, :] = jnp.where(low, 0.0, kp).astype(BF16)
            v_ref[0, b, p, 0:MEM_LEN, :] = jnp.where(low, vp, 0.0).astype(BF16)
            v_ref[0, b, p, MEM_LEN:2 * MEM_LEN, :] = jnp.where(low, 0.0, vp).astype(BF16)


def _mem_kv(mem, norm_mem_g, w_mem_kv):
    depth, batch = w_mem_kv.shape[0], mem.shape[0]
    out = jax.ShapeDtypeStruct((depth, batch, X_HEADS // 2, 2 * MEM_LEN, LANES), BF16)
    out_spec = pl.BlockSpec((1, batch, X_HEADS // 2, 2 * MEM_LEN, LANES), lambda l: (l, 0, 0, 0, 0))
    return pl.pallas_call(
        _mem_kv_kernel,
        grid=(depth,),
        in_specs=[
            pl.BlockSpec((batch, MEM_LEN, D_MODEL), lambda l: (0, 0, 0)),
            pl.BlockSpec((1, 1, D_MODEL), lambda l: (l, 0, 0)),
            pl.BlockSpec((1, D_MODEL, 2 * X_Q), lambda l: (l, 0, 0)),
        ],
        out_specs=[out_spec, out_spec],
        out_shape=[out, out],
        compiler_params=pltpu.CompilerParams(dimension_semantics=("arbitrary",),
                                             vmem_limit_bytes=VMEM_LIMIT),
    )(mem, norm_mem_g.reshape(depth, 1, D_MODEL), w_mem_kv.astype(BF16))


def _cross_attention(xq_ref, mkk_ref, mvv_ref, cat_ref, col0, rows):
    low = _lane_halves((rows.stop - rows.start, LANES))
    items = [(p, half) for p in range(X_HEADS // 2) for half in range(2)]
    mem_rows = lambda half: slice(half * MEM_LEN, (half + 1) * MEM_LEN)
    sc = [_dot_nt(xq_ref[rows, p * LANES:(p + 1) * LANES], mkk_ref[0, 0, p, mem_rows(half), :])
          for p, half in items]
    m = [jnp.max(s, axis=-1, keepdims=True) for s in sc]
    e = [jnp.exp(s - mx) for s, mx in zip(sc, m)]
    recip = [1.0 / jnp.sum(x, axis=-1, keepdims=True) for x in e]
    pv = [_dot(x.astype(BF16), mvv_ref[0, 0, p, mem_rows(half), :]) for x, (p, half) in zip(e, items)]
    for p in range(X_HEADS // 2):
        o = (pv[2 * p] + pv[2 * p + 1]) * jnp.where(low, recip[2 * p], recip[2 * p + 1])
        cat_ref[rows, col0 + p * LANES:col0 + (p + 1) * LANES] = o.astype(BF16)


def _next_tile_map(batch, n_s):
    return lambda b, s: (jnp.minimum(b + (s + 1) // n_s, batch - 1), (s + 1) % n_s, 0)


def _t5_bucket_table():
    assert WINDOW == BLOCK
    qi = np.arange(BLOCK)[:, None]
    kj = np.arange(BLOCK)[None, :]
    n = np.where(kj <= qi, qi - kj, BLOCK + qi - kj)
    max_exact = N_BUCKETS // 2
    nf = np.maximum(n, 1).astype(np.float32)
    large = max_exact + (np.log(nf / np.float32(max_exact)) / np.float32(math.log(MAX_DISTANCE / max_exact))
                         * np.float32(N_BUCKETS - max_exact)).astype(np.int32)
    large = np.minimum(large, N_BUCKETS - 1)
    return np.where(n < max_exact, n, large).astype(np.int32)


def _mixer_a_kernel(rel_ref, sink_ref, x_ref, xnext_ref, g_ref, win_ref, wout_ref, mkk_ref,
                    mvv_ref, bucket_ref, wgu_ref, wd_ref, o_ref, wgu_bf_ref, wd_bf_ref,
                    bias_ref, pnext_ref, xnn_ref, q_ref, kbuf_ref, vbuf_ref, xq_ref, cat_ref):
    ts = x_ref.shape[1]
    b_idx, s_idx = pl.program_id(0), pl.program_id(1)
    wgu_bf_ref[...] = wgu_ref[...].astype(BF16)
    wd_bf_ref[...] = wd_ref[...].astype(BF16)

    @pl.when((b_idx == 0) & (s_idx == 0))
    def _build_bias():
        bucket = bucket_ref[...]
        from_prev = (lax.broadcasted_iota(jnp.int32, (BLOCK, BLOCK), 1)
                     > lax.broadcasted_iota(jnp.int32, (BLOCK, BLOCK), 0))

        def head(h, carry):
            acc = jnp.zeros((BLOCK, BLOCK), F32)
            for bk in range(N_BUCKETS):
                acc = jnp.where(bucket == bk, rel_ref[bk, h], acc)
            bias_ref[0, h] = acc
            bias_ref[1, h] = jnp.where(from_prev, NEG_INF, acc)
            return carry

        lax.fori_loop(0, A_HEADS, head, 0)

    @pl.when(s_idx == 0)
    def _reset_carry():
        zeros = jnp.zeros((2 * A_KV_HEADS, BLOCK, LANES), BF16)
        kbuf_ref[:, 0:BLOCK, :] = zeros
        vbuf_ref[:, 0:BLOCK, :] = zeros

    @pl.when((b_idx == 0) & (s_idx == 0))
    def _first_projection():
        pnext_ref[...] = _dot(_rms(x_ref[0], g_ref[...]).astype(BF16), win_ref[...])

    low_ts = _lane_halves((ts, LANES))
    q_ref[...] = (pnext_ref[:, 0:A_Q] * (A_HEAD_DIM ** -0.5)).astype(BF16)
    xq_ref[...] = (pnext_ref[:, A_Q + 2 * A_KV:IN_A] * (X_HEAD_DIM ** -0.5)).astype(BF16)
    for src, dst in ((pnext_ref[:, A_Q:A_Q + A_KV], kbuf_ref),
                     (pnext_ref[:, A_Q + A_KV:A_Q + 2 * A_KV], vbuf_ref)):
        swapped = pltpu.roll(src, LANES // 2, axis=1)
        dst[0, BLOCK:BLOCK + ts, :] = jnp.where(low_ts, src, 0.0).astype(BF16)
        dst[1, BLOCK:BLOCK + ts, :] = jnp.where(low_ts, 0.0, swapped).astype(BF16)
        dst[2, BLOCK:BLOCK + ts, :] = jnp.where(low_ts, swapped, 0.0).astype(BF16)
        dst[3, BLOCK:BLOCK + ts, :] = jnp.where(low_ts, 0.0, src).astype(BF16)
    xnn_ref[...] = _rms(xnext_ref[0], g_ref[...]).astype(BF16)

    n_groups = (ts // BLOCK) * A_HEADS // A_ITEMS
    stages_total = 5 * n_groups
    issued = [0, 0]

    def after_stage():
        issued[0] += 1
        while issued[1] < (issued[0] * (IN_A // PROJ_COLS)) // stages_total:
            cols = slice(issued[1] * PROJ_COLS, (issued[1] + 1) * PROJ_COLS)
            pnext_ref[:, cols] = _dot(xnn_ref[...], win_ref[:, cols])
            issued[1] += 1

    low_blk = _lane_halves((BLOCK, LANES))
    from_cur = (lax.broadcasted_iota(jnp.int32, (BLOCK, BLOCK), 1)
                <= lax.broadcasted_iota(jnp.int32, (BLOCK, BLOCK), 0))
    first = jnp.where(s_idx == 0, 1, 0)
    tiles_per_kv = A_GROUP // 2
    all_items = [(i, j, half) for i in range(ts // BLOCK) for j in range(A_HEADS // 2)
                 for half in range(2)]
    band = lambda i: slice(i * BLOCK, (i + 2) * BLOCK)
    kv_slot = lambda j, half: 2 * (j // tiles_per_kv) + half
    for g in range(0, len(all_items), A_ITEMS):
        items = all_items[g:g + A_ITEMS]
        sinks = [sink_ref[0, 2 * j + half] for i, j, half in items]
        band_sc = [_dot_nt(q_ref[i * BLOCK:(i + 1) * BLOCK, j * LANES:(j + 1) * LANES],
                           kbuf_ref[kv_slot(j, half), band(i), :]) for i, j, half in items]
        sc = [jnp.where(from_cur, b[:, BLOCK:2 * BLOCK], b[:, 0:BLOCK])
              + bias_ref[first if i == 0 else 0, 2 * j + half]
              for b, (i, j, half) in zip(band_sc, items)]
        after_stage()
        m = [jnp.maximum(jnp.max(s, axis=-1, keepdims=True), sink) for s, sink in zip(sc, sinks)]
        after_stage()
        p = [jnp.exp(s - mx) for s, mx in zip(sc, m)]
        after_stage()
        den = [jnp.sum(pr, axis=-1, keepdims=True) + jnp.exp(sink - mx)
               for pr, sink, mx in zip(p, sinks, m)]
        after_stage()
        pv = [_dot(jnp.concatenate([jnp.where(from_cur, 0.0, pr), jnp.where(from_cur, pr, 0.0)],
                                   axis=1).astype(BF16),
                   vbuf_ref[kv_slot(j, half), band(i), :])
              for pr, (i, j, half) in zip(p, items)]
        after_stage()
        recip = [1.0 / d for d in den]
        for n in range(0, len(items), 2):
            i, j, _ = items[n]
            o = (pv[n] + pv[n + 1]) * jnp.where(low_blk, recip[n], recip[n + 1])
            cat_ref[i * BLOCK:(i + 1) * BLOCK, j * LANES:(j + 1) * LANES] = o.astype(BF16)

    kbuf_ref[:, 0:BLOCK, :] = kbuf_ref[:, ts:ts + BLOCK, :]
    vbuf_ref[:, 0:BLOCK, :] = vbuf_ref[:, ts:ts + BLOCK, :]

    _cross_attention(xq_ref, mkk_ref, mvv_ref, cat_ref, A_Q, slice(0, ts))
    o_ref[0] = x_ref[0] + _dot(cat_ref[...], wout_ref[0])


def _const_spec(shape):
    n = len(shape)
    return pl.BlockSpec(shape, lambda b, s: (0,) * n)


def _mixer_a(h, layer, rel_bias, sinks, norm_g, w_in, w_out, mkk, mvv, w_gate_up, w_down):
    batch, seq, _ = h.shape
    ts = A_TILE
    assert IN_A % PROJ_COLS == 0
    n_steps = batch * (seq // ts)
    depth = w_gate_up.shape[0]
    gu_slabs = n_steps // depth
    gu_rows = D_MODEL // gu_slabs
    wd_rows = 2 * SUBLANES * 11
    wd_slabs = D_FF // wd_rows
    assert gu_rows % (2 * SUBLANES) == 0 and D_MODEL % gu_slabs == 0 and D_FF % wd_rows == 0
    assert depth * wd_slabs <= n_steps
    step = lambda b, s: b * (seq // ts) + s
    gu_map = lambda b, s: (step(b, s) // gu_slabs, step(b, s) % gu_slabs, 0)
    wd_step = lambda b, s: jnp.minimum(step(b, s), depth * wd_slabs - 1)
    wd_map = lambda b, s: (wd_step(b, s) // wd_slabs, wd_step(b, s) % wd_slabs, 0)
    bucket = jnp.asarray(_t5_bucket_table())
    mem_spec = pl.BlockSpec((1, 1, X_HEADS // 2, 2 * MEM_LEN, LANES), lambda b, s: (layer, b, 0, 0, 0))
    return pl.pallas_call(
        _mixer_a_kernel,
        grid=(batch, seq // ts),
        in_specs=[
            pl.BlockSpec(memory_space=pltpu.SMEM),
            pl.BlockSpec(memory_space=pltpu.SMEM),
            pl.BlockSpec((1, ts, D_MODEL), lambda b, s: (b, s, 0)),
            pl.BlockSpec((1, ts, D_MODEL), _next_tile_map(batch, seq // ts)),
            _const_spec((1, D_MODEL)),
            _const_spec((D_MODEL, IN_A)),
            pl.BlockSpec((1, A_Q + X_Q, D_MODEL), lambda b, s: (layer, 0, 0)),
            mem_spec, mem_spec,
            _const_spec((BLOCK, BLOCK)),
            pl.BlockSpec((1, gu_rows, 2 * D_FF), gu_map),
            pl.BlockSpec((1, wd_rows, D_MODEL), wd_map),
        ],
        out_specs=[
            pl.BlockSpec((1, ts, D_MODEL), lambda b, s: (b, s, 0)),
            pl.BlockSpec((1, gu_rows, 2 * D_FF), gu_map),
            pl.BlockSpec((1, wd_rows, D_MODEL), wd_map),
        ],
        out_shape=[
            jax.ShapeDtypeStruct(h.shape, F32),
            jax.ShapeDtypeStruct(w_gate_up.shape, BF16),
            jax.ShapeDtypeStruct(w_down.shape, BF16),
        ],
        scratch_shapes=[
            pltpu.VMEM((2, A_HEADS, BLOCK, BLOCK), F32),
            pltpu.VMEM((ts, IN_A), F32),
            pltpu.VMEM((ts, D_MODEL), BF16),
            pltpu.VMEM((ts, A_Q), BF16),
            pltpu.VMEM((2 * A_KV_HEADS, ts + BLOCK, LANES), BF16),
            pltpu.VMEM((2 * A_KV_HEADS, ts + BLOCK, LANES), BF16),
            pltpu.VMEM((ts, X_Q), BF16),
            pltpu.VMEM((ts, A_Q + X_Q), BF16),
        ],
        compiler_params=pltpu.CompilerParams(dimension_semantics=("arbitrary", "arbitrary"),
                                             vmem_limit_bytes=VMEM_LIMIT),
    )(rel_bias, sinks.reshape(1, A_HEADS), h, h, norm_g.reshape(1, D_MODEL), w_in.astype(BF16),
      w_out, mkk, mvv, bucket, w_gate_up, w_down)


def _ffn_kernel(x_ref, xnext_ref, g_ref, wgu_ref, cw_ref, cb_ref, wd_ref, fg_ref, o_ref,
                xn_ref, xnn_ref, gbuf_ref, ubuf_ref, tail_ref, acc_ref, *, final_norm):
    ts = x_ref.shape[1]
    n_chunks = D_FF // FF_CHUNK
    cols_of = lambda c: slice(c * FF_CHUNK, (c + 1) * FF_CHUNK)
    slot_of = lambda c: 0 if c == 0 else 1 + c % 2

    def project(c, src_ref):
        xn = src_ref[...]
        gbuf_ref[slot_of(c), SUBLANES:SUBLANES + ts, :] = _dot(xn, wgu_ref[0, :, cols_of(c)])
        ubuf_ref[slot_of(c)] = _dot(xn, wgu_ref[0, :, D_FF + c * FF_CHUNK:D_FF + (c + 1) * FF_CHUNK])

    @pl.when(pl.program_id(1) == 0)
    def _reset_tail():
        tail_ref[...] = jnp.zeros(tail_ref.shape, F32)

    @pl.when((pl.program_id(0) == 0) & (pl.program_id(1) == 0))
    def _first_tile():
        xnn_ref[...] = _rms(x_ref[0], g_ref[...]).astype(BF16)
        project(0, xnn_ref)

    xn_ref[...] = xnn_ref[...]
    acc_ref[...] = x_ref[0]
    for c in range(n_chunks):
        slot = slot_of(c)
        gbuf_ref[slot, 0:SUBLANES, :] = tail_ref[:, cols_of(c)]
        cw = cw_ref[:, cols_of(c)]
        y = (cw[2:3, :] * gbuf_ref[slot, SUBLANES:SUBLANES + ts, :]
             + cw[1:2, :] * gbuf_ref[slot, SUBLANES - 1:SUBLANES - 1 + ts, :]
             + cw[0:1, :] * gbuf_ref[slot, SUBLANES - 2:SUBLANES - 2 + ts, :]
             + cb_ref[:, cols_of(c)])
        tail_ref[:, cols_of(c)] = gbuf_ref[slot, ts:ts + SUBLANES, :]
        act = (_silu(y) * ubuf_ref[slot]).astype(BF16)
        if c + 1 < n_chunks:
            project(c + 1, xn_ref)
        else:
            project(0, xnn_ref)
        acc_ref[...] += _dot(act, wd_ref[0, cols_of(c), :])
        if c == n_chunks // 2:
            xnn_ref[...] = _rms(xnext_ref[0], g_ref[...]).astype(BF16)
    out = acc_ref[...]
    if final_norm:
        out = _rms(out, fg_ref[...])
    o_ref[0] = out


def _ffn(h, layer, norm_g, w_gate_up, conv_w, conv_b, w_down, final_g, final_norm):
    batch, seq, _ = h.shape
    ts = FFN_TILE
    return pl.pallas_call(
        functools.partial(_ffn_kernel, final_norm=final_norm),
        grid=(batch, seq // ts),
        in_specs=[
            pl.BlockSpec((1, ts, D_MODEL), lambda b, s: (b, s, 0)),
            pl.BlockSpec((1, ts, D_MODEL), _next_tile_map(batch, seq // ts)),
            _const_spec((1, D_MODEL)),
            pl.BlockSpec((1, D_MODEL, 2 * D_FF), lambda b, s: (layer, 0, 0)),
            _const_spec((FFN_CONV, D_FF)),
            _const_spec((1, D_FF)),
            pl.BlockSpec((1, D_FF, D_MODEL), lambda b, s: (layer, 0, 0)),
            _const_spec((1, D_MODEL)),
        ],
        out_specs=pl.BlockSpec((1, ts, D_MODEL), lambda b, s: (b, s, 0)),
        out_shape=jax.ShapeDtypeStruct(h.shape, F32),
        scratch_shapes=[
            pltpu.VMEM((ts, D_MODEL), BF16),
            pltpu.VMEM((ts, D_MODEL), BF16),
            pltpu.VMEM((3, ts + SUBLANES, FF_CHUNK), F32),
            pltpu.VMEM((3, ts, FF_CHUNK), F32),
            pltpu.VMEM((SUBLANES, D_FF), F32),
            pltpu.VMEM((ts, D_MODEL), F32),
        ],
        compiler_params=pltpu.CompilerParams(dimension_semantics=("arbitrary", "arbitrary"),
                                             vmem_limit_bytes=VMEM_LIMIT),
    )(h, h, norm_g.reshape(1, D_MODEL), w_gate_up, conv_w, conv_b.reshape(1, D_FF), w_down,
      final_g.reshape(1, D_MODEL))


def _stack_diag(top, bottom):
    zero = jnp.zeros_like(top)
    return jnp.concatenate([jnp.concatenate([top, zero], axis=1),
                            jnp.concatenate([zero, bottom], axis=1)], axis=0)


def _pair_blocks(m, low):
    zero = jnp.zeros_like(m)
    return jnp.concatenate([jnp.where(low, m, zero), jnp.where(low, zero, m)], axis=0)


def _split_bf16(a):
    hi = a.astype(BF16)
    return hi, (a - hi.astype(F32)).astype(BF16)


def _pair_product(l, r, low):
    lh, ll = _split_bf16(l)
    rh, rl = _split_bf16(r)
    rbh, rbl = _pair_blocks(rh, low), _pair_blocks(rl, low)
    return _dot(lh, rbh) + (_dot(lh, rbl) + _dot(ll, rbh))


def _pair_mm(ls, rs, low, split=False):
    if split:
        return [_pair_product(l, r, low) for l, r in zip(ls, rs)]
    return [_dot(l.astype(BF16), _pair_blocks(r.astype(BF16), low)) for l, r in zip(ls, rs)]


def _pair_mm_stacked(tops, bottoms, rs, low):
    out = [_pair_product(jnp.concatenate([t, b], axis=0), r, low)
           for t, b, r in zip(tops, bottoms, rs)]
    return [x[0:CHUNK, :] for x in out], [x[CHUNK:2 * CHUNK, :] for x in out]


def _pair_unit_lower_inverses(a, eye, diag_blocks, low):
    d = [jnp.where(diag_blocks, x, 0.0) for x in a]
    e = [x - y for x, y in zip(a, d)]
    x0 = [eye - x for x in d]
    d2 = _pair_mm(d, d, low, split=True)
    t, d4 = _pair_mm_stacked(x0, d2, d2, low)
    x1 = [p + q for p, q in zip(x0, t)]
    t, d8 = _pair_mm_stacked(x1, d4, d4, low)
    x2 = [p + q for p, q in zip(x1, t)]
    td = [p + q for p, q in zip(x2, _pair_mm(x2, d8, low, split=True))]
    n = _pair_mm(td, e, low)
    n2 = _pair_mm(n, n, low)
    v = [p - q for p, q in zip(td, _pair_mm(n, td, low))]
    return [p + q for p, q in zip(v, _pair_mm(n2, v, low))]


def _mixer_b_kernel(x_ref, g_ref, win_ref, cw_ref, alog_ref, dtb_ref, ong_ref, wout_ref,
                    mkk_ref, mvv_ref, o_ref,
                    cbuf_ref, q_ref, k_ref, v_ref, qd_ref, ktb_ref, z_ref, gcol_ref, grow_ref,
                    brow_ref, egrow_ref, dc_ref, xq_ref, osc_ref, state_ref, cat_ref):
    ts = x_ref.shape[1]
    n_pairs = B_V_HEADS // 2
    per_block = LANES // CHUNK
    sub = CHUNK_GROUP * CHUNK
    n_sub = ts // sub

    @pl.when(pl.program_id(1) == 0)
    def _reset_carry():
        cbuf_ref[ts:ts + SUBLANES, :] = jnp.zeros((SUBLANES, B_QKV), F32)
        state_ref[...] = jnp.zeros(state_ref.shape, F32)

    r_t = lax.broadcasted_iota(jnp.int32, (LANES, LANES), 0)
    c_t = lax.broadcasted_iota(jnp.int32, (LANES, LANES), 1)
    same_chunk_tri = jnp.where(r_t >= c_t, r_t // CHUNK, -1) == (c_t // CHUNK)
    tri = jnp.where(same_chunk_tri, 1.0, 0.0).astype(BF16)
    low_sub = _lane_halves((sub, LANES))
    low8 = _lane_halves((SUBLANES, LANES))
    low_row = _lane_halves((1, LANES))
    cw = cw_ref[...]
    cbuf_ref[0:SUBLANES, :] = cbuf_ref[ts:ts + SUBLANES, :]

    def prepare(r):
        rows = slice(r * sub, (r + 1) * sub)
        xn = _rms(x_ref[0, rows, :], g_ref[...]).astype(BF16)
        base = SUBLANES + r * sub
        ba0 = B_QKV + B_V + X_Q
        gates = _dot(xn, win_ref[:, ba0:ba0 + 2 * LANES])
        cbuf_ref[base:base + sub, :] = _dot(xn, win_ref[:, 0:B_QKV])
        z_ref[rows, :] = _dot(xn, win_ref[:, B_QKV:B_QKV + B_V])
        xq_ref[rows, :] = _dot(xn, win_ref[:, B_QKV + B_V:ba0]).astype(BF16)
        beta = 1.0 / (1.0 + jnp.exp(-gates[:, 0:LANES]))
        a_logit = gates[:, LANES:2 * LANES] + dtb_ref[...]
        softplus = jnp.maximum(a_logit, 0.0) + jnp.log1p(jnp.exp(-jnp.abs(a_logit)))
        gl = -jnp.exp(alog_ref[...]) * softplus

        g_hi = gl.astype(BF16)
        g_rest = gl - g_hi.astype(F32)
        g_mid = g_rest.astype(BF16)
        g_lo = (g_rest - g_mid.astype(F32)).astype(BF16)
        gc = jnp.concatenate(
            [_dot(tri, g_hi[blk]) + (_dot(tri, g_mid[blk]) + _dot(tri, g_lo[blk]))
             for blk in (slice(i * LANES, (i + 1) * LANES) for i in range(sub // LANES))],
            axis=0)
        eg = jnp.exp(gc)

        def conv_silu(c0):
            cols = slice(c0, c0 + LANES)
            y = cw[B_CONV - 1:B_CONV, cols] * cbuf_ref[base:base + sub, cols]
            for tap in range(1, B_CONV):
                y = y + (cw[B_CONV - 1 - tap:B_CONV - tap, cols]
                         * cbuf_ref[base - tap:base - tap + sub, cols])
            return _silu(y)

        k_heads = []
        for hh in range(B_QK_HEADS):
            cols = slice(hh * LANES, (hh + 1) * LANES)
            qh = conv_silu(hh * LANES)
            kh = conv_silu(B_QK + hh * LANES)
            qh = qh * lax.rsqrt(jnp.sum(qh * qh, axis=-1, keepdims=True) + EPS) * (B_HEAD_DIM ** -0.5)
            kh = kh * lax.rsqrt(jnp.sum(kh * kh, axis=-1, keepdims=True) + EPS)
            q_ref[rows, cols] = qh.astype(BF16)
            k_ref[rows, cols] = kh.astype(BF16)
            k_heads.append(kh)
            for h in (2 * hh, 2 * hh + 1):
                qd_ref[rows, h * LANES:(h + 1) * LANES] = (qh * eg[:, h:h + 1]).astype(BF16)
        for h in range(B_V_HEADS):
            v_ref[rows, h * LANES:(h + 1) * LANES] = conv_silu(2 * B_QK + h * LANES).astype(BF16)
        for p in range(n_pairs):
            gcol_ref[p, rows, :] = jnp.where(low_sub, gc[:, 2 * p:2 * p + 1],
                                             gc[:, 2 * p + 1:2 * p + 2])

        for blk in range(sub // LANES):
            loc = slice(blk * LANES, (blk + 1) * LANES)
            tok = slice(r * sub + blk * LANES, r * sub + (blk + 1) * LANES)
            g8 = gc[loc, :].T[0:SUBLANES, :]
            b8 = beta[loc, :].T[0:SUBLANES, :]
            e8 = jnp.exp(g8)
            g_last = [g8[:, (half + 1) * CHUNK - 1:(half + 1) * CHUNK] for half in range(per_block)]
            g_last8 = jnp.where(low8, g_last[0], g_last[1])
            kt8 = jnp.exp(g_last8 - g8) * b8
            swapped = [pltpu.roll(t, CHUNK, axis=1) for t in (g8, b8, e8, kt8)]
            for half in range(per_block):
                c = (r * sub + blk * LANES) // CHUNK + half
                dc_ref[c] = jnp.exp(jnp.broadcast_to(g_last[half], (SUBLANES, LANES)))
                for p in range(n_pairs):
                    pair_rows = [jnp.where(low_row,
                                           (t if half == 0 else t_sw)[2 * p:2 * p + 1, :],
                                           (t_sw if half == 0 else t)[2 * p + 1:2 * p + 2, :])
                                 for t, t_sw in zip((g8, b8, e8, kt8), swapped)]
                    grow_ref[c, p:p + 1, :] = pair_rows[0]
                    brow_ref[c, p:p + 1, :] = pair_rows[1]
                    egrow_ref[c, p:p + 1, :] = pair_rows[2]
                    k_c = k_heads[p][blk * LANES + half * CHUNK:blk * LANES + (half + 1) * CHUNK, :]
                    k_twice_t = jnp.concatenate([k_c, k_c], axis=0).T
                    ktb_ref[c, p] = (k_twice_t * pair_rows[3]).astype(BF16)

    row = lax.broadcasted_iota(jnp.int32, (CHUNK, LANES), 0)
    col = lax.broadcasted_iota(jnp.int32, (CHUNK, LANES), 1) % CHUNK
    low = _lane_halves((CHUNK, LANES))
    incl = row >= col
    strict = row > col
    eye = jnp.where(row == col, 1.0, 0.0)
    diag_blocks = (row // SUB) == (col // SUB)
    rows_of = lambda c: slice(c * CHUNK, (c + 1) * CHUNK)
    cols_of = lambda i: slice(i * LANES, (i + 1) * LANES)

    def delta_rule(r, states):
        chunks = range(r * CHUNK_GROUP, (r + 1) * CHUNK_GROUP)
        def state_free(items):
            k_c = [k_ref[rows_of(c), cols_of(p)] for c, p in items]
            k_twice = [jnp.concatenate([kc, kc], axis=0) for kc in k_c]
            kq = [_dot_nt(jnp.concatenate([kc, q_ref[rows_of(c), cols_of(p)]], axis=0), k2)
                  for (c, p), kc, k2 in zip(items, k_c, k_twice)]
            kk = [x[0:CHUNK, :] for x in kq]
            qk = [x[CHUNK:2 * CHUNK, :] for x in kq]
            decay = [jnp.exp(jnp.where(incl, gcol_ref[p, rows_of(c), :] - grow_ref[c, p:p + 1, :],
                                       NEG_INF)) for c, p in items]
            b_row = [brow_ref[c, p:p + 1, :] for c, p in items]
            a = [kk_i * jnp.where(strict, d_i, 0.0) * b_i for kk_i, d_i, b_i in zip(kk, decay, b_row)]
            attn = [(qk_i * d_i * b_i).astype(BF16) for qk_i, d_i, b_i in zip(qk, decay, b_row)]
            t_inv = _pair_unit_lower_inverses(a, eye, diag_blocks, low)
            u = [_dot(t_i.astype(BF16), _stack_diag(v_ref[rows_of(c), cols_of(2 * p)],
                                                    v_ref[rows_of(c), cols_of(2 * p + 1)]))
                 for t_i, (c, p) in zip(t_inv, items)]
            w = [_dot((t_i * egrow_ref[c, p:p + 1, :]).astype(BF16), _stack_diag(kc, kc)).astype(BF16)
                 for t_i, (c, p), kc in zip(t_inv, items, k_c)]
            return attn, u, w

        chunks_per_group = PHASE_ITEMS // n_pairs
        for g0 in range(0, CHUNK_GROUP, chunks_per_group):
            group = chunks[g0:g0 + chunks_per_group]
            attn, u, w = state_free([(c, p) for c in group for p in range(n_pairs)])
            for ci, c in enumerate(group):
                pairs = range(n_pairs)
                w_s = [_dot(jnp.concatenate([w[ci * n_pairs + p],
                                             qd_ref[rows_of(c), 2 * p * LANES:(2 * p + 2) * LANES]],
                                            axis=0),
                            _stack_diag(states[2 * p].astype(BF16), states[2 * p + 1].astype(BF16)))
                       for p in pairs]
                delta = [(u[ci * n_pairs + p] - w_s[p][0:CHUNK, :]).astype(BF16) for p in pairs]
                upd = [_dot(jnp.concatenate([ktb_ref[c, p], attn[ci * n_pairs + p]], axis=0),
                            _stack_diag(delta[p][:, 0:LANES], delta[p][:, LANES:2 * LANES]))
                       for p in pairs]
                for p in pairs:
                    for i, h in enumerate((2 * p, 2 * p + 1)):
                        states[h] = (dc_ref[c, h:h + 1, :] * states[h]
                                     + upd[p][0:B_HEAD_DIM, cols_of(i)])
                    osc_ref[rows_of(c), 2 * p * LANES:(2 * p + 2) * LANES] = (
                        w_s[p][CHUNK:, :] + upd[p][B_HEAD_DIM:, :])
        return states

    def finish(r):
        rows = slice(r * sub, (r + 1) * sub)
        ong = ong_ref[...]
        for h in range(B_V_HEADS):
            o = osc_ref[rows, cols_of(h)]
            o = o * lax.rsqrt(jnp.mean(o * o, axis=-1, keepdims=True) + EPS) * ong
            cat_ref[rows, cols_of(h)] = (o * _silu(z_ref[rows, cols_of(h)])).astype(BF16)
        _cross_attention(xq_ref, mkk_ref, mvv_ref, cat_ref, B_V, rows)
        o_ref[0, rows, :] = x_ref[0, rows, :] + _dot(cat_ref[rows, :], wout_ref[0])

    for r in range(n_sub):
        prepare(r)
    states = [state_ref[h] for h in range(B_V_HEADS)]
    for r in range(n_sub):
        states = delta_rule(r, states)
    for h in range(B_V_HEADS):
        state_ref[h] = states[h]
    for r in range(n_sub):
        finish(r)


def _mixer_b(h, layer, norm_g, w_in, conv_w, a_log, dt_bias, out_norm_g, w_out, mkk, mvv):
    batch, seq, _ = h.shape
    ts = B_TILE
    c0 = B_QKV + B_V
    w_bf = w_in.astype(BF16)
    pad = jnp.zeros((D_MODEL, LANES - B_V_HEADS), BF16)
    w_in_k = jnp.concatenate([
        w_bf[:, 0:c0],
        w_bf[:, IN_B - X_Q:] * (X_HEAD_DIM ** -0.5),
        w_bf[:, c0:c0 + B_V_HEADS], pad,
        w_bf[:, c0 + B_V_HEADS:c0 + 2 * B_V_HEADS], pad], axis=1)
    lane_pad = lambda t: jnp.pad(t.reshape(1, B_V_HEADS), ((0, 0), (0, LANES - B_V_HEADS)))
    mem_spec = pl.BlockSpec((1, 1, X_HEADS // 2, 2 * MEM_LEN, LANES), lambda b, s: (layer, b, 0, 0, 0))
    n_chunks = ts // CHUNK
    return pl.pallas_call(
        _mixer_b_kernel,
        grid=(batch, seq // ts),
        in_specs=[
            pl.BlockSpec((1, ts, D_MODEL), lambda b, s: (b, s, 0)),
            _const_spec((1, D_MODEL)),
            _const_spec((D_MODEL, IN_B_PAD)),
            _const_spec((B_CONV, B_QKV)),
            _const_spec((1, LANES)),
            _const_spec((1, LANES)),
            _const_spec((1, B_HEAD_DIM)),
            pl.BlockSpec((1, B_V + X_Q, D_MODEL), lambda b, s: (layer, 0, 0)),
            mem_spec, mem_spec,
        ],
        out_specs=pl.BlockSpec((1, ts, D_MODEL), lambda b, s: (b, s, 0)),
        out_shape=jax.ShapeDtypeStruct(h.shape, F32),
        scratch_shapes=[
            pltpu.VMEM((ts + SUBLANES, B_QKV), F32),
            pltpu.VMEM((ts, B_QK), BF16),
            pltpu.VMEM((ts, B_QK), BF16),
            pltpu.VMEM((ts, B_V), BF16),
            pltpu.VMEM((ts, B_V), BF16),
            pltpu.VMEM((n_chunks, B_V_HEADS // 2, B_HEAD_DIM, LANES), BF16),
            pltpu.VMEM((ts, B_V), F32),
            pltpu.VMEM((B_V_HEADS // 2, ts, LANES), F32),
            pltpu.VMEM((n_chunks, SUBLANES, LANES), F32),
            pltpu.VMEM((n_chunks, SUBLANES, LANES), F32),
            pltpu.VMEM((n_chunks, SUBLANES, LANES), F32),
            pltpu.VMEM((n_chunks, SUBLANES, LANES), F32),
            pltpu.VMEM((ts, X_Q), BF16),
            pltpu.VMEM((ts, B_V), F32),
            pltpu.VMEM((B_V_HEADS, B_HEAD_DIM, B_HEAD_DIM), F32),
            pltpu.VMEM((ts, B_V + X_Q), BF16),
        ],
        compiler_params=pltpu.CompilerParams(dimension_semantics=("arbitrary", "arbitrary"),
                                             vmem_limit_bytes=VMEM_LIMIT),
    )(h, norm_g.reshape(1, D_MODEL), w_in_k, conv_w, lane_pad(a_log), lane_pad(dt_bias),
      out_norm_g.reshape(1, B_HEAD_DIM), w_out, mkk, mvv)


def kernel(x, mem, rel_bias, norm_mix_g, norm_mem_g, w_mem_kv, w_out, w_in_a, sinks_a, w_in_b,
           conv_qkv_b, a_log_b, dt_bias_b, out_norm_g_b, norm_ffn_g, w_gate_up, ffn_conv_w,
           ffn_conv_b, w_down, final_norm_g):
    mkk, mvv = _mem_kv(mem, norm_mem_g, w_mem_kv)
    w_out_bf = w_out.astype(BF16)
    h, w_gate_up_bf, w_down_bf = _mixer_a(x, 0, rel_bias, sinks_a[0], norm_mix_g[0], w_in_a[0],
                                          w_out_bf, mkk, mvv, w_gate_up, w_down)
    h = _ffn(h, 0, norm_ffn_g[0], w_gate_up_bf, ffn_conv_w[0], ffn_conv_b[0], w_down_bf,
             final_norm_g, False)
    h = _mixer_b(h, 1, norm_mix_g[1], w_in_b[0], conv_qkv_b[0], a_log_b[0], dt_bias_b[0],
                 out_norm_g_b[0], w_out_bf, mkk, mvv)
    h = _ffn(h, 1, norm_ffn_g[1], w_gate_up_bf, ffn_conv_w[1], ffn_conv_b[1], w_down_bf,
             final_norm_g, True)
    return h
```

```python
import functools
import math

import numpy as np
import jax
import jax.numpy as jnp
from jax import lax
from jax.experimental import pallas as pl
from jax.experimental.pallas import tpu as pltpu

D_MODEL = 1024
MEM_LEN = 256
EPS = 1e-6

A_HEADS = 12
A_KV_HEADS = 2
A_HEAD_DIM = 64
A_GROUP = A_HEADS // A_KV_HEADS
WINDOW = 128
BLOCK = 128
N_BUCKETS = 32
MAX_DISTANCE = 128

B_QK_HEADS = 3
B_V_HEADS = 6
B_HEAD_DIM = 128
B_CONV = 4
CHUNK = 64
SUB = 16

X_HEADS = 4
X_HEAD_DIM = 64

D_FF = 2816
FFN_CONV = 3

A_Q = A_HEADS * A_HEAD_DIM
A_KV = A_KV_HEADS * A_HEAD_DIM
X_Q = X_HEADS * X_HEAD_DIM
B_QK = B_QK_HEADS * B_HEAD_DIM
B_V = B_V_HEADS * B_HEAD_DIM
B_QKV = 2 * B_QK + B_V
IN_A = A_Q + 2 * A_KV + X_Q
IN_B = B_QKV + B_V + 2 * B_V_HEADS + X_Q

LANES = 128
SUBLANES = 8
IN_B_PAD = B_QKV + B_V + X_Q + 2 * LANES

A_TILE = 512
B_TILE = 512
FFN_TILE = 256
FF_CHUNK = 256
CHUNK_GROUP = 8
A_ITEMS = 24
PHASE_ITEMS = 12
PROJ_COLS = 256
VMEM_LIMIT = 56 * 1024 * 1024

F32 = jnp.float32
BF16 = jnp.bfloat16
NEG_INF = float("-inf")


def _dot(a, b, precision=None):
    return jnp.dot(a, b, preferred_element_type=F32, precision=precision)


def _dot_nt(a, b, precision=None):
    return lax.dot_general(a, b, (((1,), (1,)), ((), ())), preferred_element_type=F32,
                           precision=precision)


def _rms(x, g):
    return x * lax.rsqrt(jnp.mean(x * x, axis=-1, keepdims=True) + EPS) * g


def _silu(x):
    return x / (1.0 + jnp.exp(-x))


def _lane_halves(shape):
    return lax.broadcasted_iota(jnp.int32, shape, len(shape) - 1) < (LANES // 2)


def _mem_kv_kernel(mem_ref, g_ref, w_ref, k_ref, v_ref):
    batch = mem_ref.shape[0]
    mn = _rms(mem_ref[...].reshape(batch * MEM_LEN, D_MODEL), g_ref[0]).astype(BF16)
    kv = _dot(mn, w_ref[0])
    low = _lane_halves((MEM_LEN, LANES))
    for b in range(batch):
        rows = slice(b * MEM_LEN, (b + 1) * MEM_LEN)
        for p in range(X_HEADS // 2):
            kp = kv[rows, p * LANES:(p + 1) * LANES]
            vp = kv[rows, X_Q + p * LANES:X_Q + (p + 1) * LANES]
            k_ref[0, b, p, 0:MEM_LEN, :] = jnp.where(low, kp, 0.0).astype(BF16)
            k_ref[0, b, p, MEM_LEN:2 * MEM_LEN, :] = jnp.where(low, 0.0, kp).astype(BF16)
            v_ref[0, b, p, 0:MEM_LEN, :] = jnp.where(low, vp, 0.0).astype(BF16)
            v_ref[0, b, p, MEM_LEN:2 * MEM_LEN, :] = jnp.where(low, 0.0, vp).astype(BF16)


def _mem_kv(mem, norm_mem_g, w_mem_kv):
    depth, batch = w_mem_kv.shape[0], mem.shape[0]
    out = jax.ShapeDtypeStruct((depth, batch, X_HEADS // 2, 2 * MEM_LEN, LANES), BF16)
    out_spec = pl.BlockSpec((1, batch, X_HEADS // 2, 2 * MEM_LEN, LANES), lambda l: (l, 0, 0, 0, 0))
    return pl.pallas_call(
        _mem_kv_kernel,
        grid=(depth,),
        in_specs=[
            pl.BlockSpec((batch, MEM_LEN, D_MODEL), lambda l: (0, 0, 0)),
            pl.BlockSpec((1, 1, D_MODEL), lambda l: (l, 0, 0)),
            pl.BlockSpec((1, D_MODEL, 2 * X_Q), lambda l: (l, 0, 0)),
        ],
        out_specs=[out_spec, out_spec],
        out_shape=[out, out],
        compiler_params=pltpu.CompilerParams(dimension_semantics=("arbitrary",),
                                             vmem_limit_bytes=VMEM_LIMIT),
    )(mem, norm_mem_g.reshape(depth, 1, D_MODEL), w_mem_kv.astype(BF16))


def _cross_attention(xq_ref, mkk_ref, mvv_ref, cat_ref, col0, rows):
    low = _lane_halves((rows.stop - rows.start, LANES))
    items = [(p, half) for p in range(X_HEADS // 2) for half in range(2)]
    mem_rows = lambda half: slice(half * MEM_LEN, (half + 1) * MEM_LEN)
    sc = [_dot_nt(xq_ref[rows, p * LANES:(p + 1) * LANES], mkk_ref[0, 0, p, mem_rows(half), :])
          for p, half in items]
    m = [jnp.max(s, axis=-1, keepdims=True) for s in sc]
    e = [jnp.exp(s - mx) for s, mx in zip(sc, m)]
    recip = [1.0 / jnp.sum(x, axis=-1, keepdims=True) for x in e]
    pv = [_dot(x.astype(BF16), mvv_ref[0, 0, p, mem_rows(half), :]) for x, (p, half) in zip(e, items)]
    for p in range(X_HEADS // 2):
        o = (pv[2 * p] + pv[2 * p + 1]) * jnp.where(low, recip[2 * p], recip[2 * p + 1])
        cat_ref[rows, col0 + p * LANES:col0 + (p + 1) * LANES] = o.astype(BF16)


def _next_tile_map(batch, n_s):
    return lambda b, s: (jnp.minimum(b + (s + 1) // n_s, batch - 1), (s + 1) % n_s, 0)


def _t5_bucket_table():
    assert WINDOW == BLOCK
    qi = np.arange(BLOCK)[:, None]
    kj = np.arange(BLOCK)[None, :]
    n = np.where(kj <= qi, qi - kj, BLOCK + qi - kj)
    max_exact = N_BUCKETS // 2
    nf = np.maximum(n, 1).astype(np.float32)
    large = max_exact + (np.log(nf / np.float32(max_exact)) / np.float32(math.log(MAX_DISTANCE / max_exact))
                         * np.float32(N_BUCKETS - max_exact)).astype(np.int32)
    large = np.minimum(large, N_BUCKETS - 1)
    return np.where(n < max_exact, n, large).astype(np.int32)


def _mixer_a_kernel(rel_ref, sink_ref, x_ref, xnext_ref, g_ref, win_ref, wout_ref, mkk_ref,
                    mvv_ref, bucket_ref, wgu_ref, wd_ref, o_ref, wgu_bf_ref, wd_bf_ref,
                    bias_ref, pnext_ref, xnn_ref, q_ref, kbuf_ref, vbuf_ref, xq_ref, cat_ref):
    ts = x_ref.shape[1]
    b_idx, s_idx = pl.program_id(0), pl.program_id(1)
    wgu_bf_ref[...] = wgu_ref[...].astype(BF16)
    wd_bf_ref[...] = wd_ref[...].astype(BF16)

    @pl.when((b_idx == 0) & (s_idx == 0))
    def _build_bias():
        bucket = bucket_ref[...]
        from_prev = (lax.broadcasted_iota(jnp.int32, (BLOCK, BLOCK), 1)
                     > lax.broadcasted_iota(jnp.int32, (BLOCK, BLOCK), 0))

        def head(h, carry):
            acc = jnp.zeros((BLOCK, BLOCK), F32)
            for bk in range(N_BUCKETS):
                acc = jnp.where(bucket == bk, rel_ref[bk, h], acc)
            bias_ref[0, h] = acc
            bias_ref[1, h] = jnp.where(from_prev, NEG_INF, acc)
            return carry

        lax.fori_loop(0, A_HEADS, head, 0)

    @pl.when(s_idx == 0)
    def _reset_carry():
        zeros = jnp.zeros((2 * A_KV_HEADS, BLOCK, LANES), BF16)
        kbuf_ref[:, 0:BLOCK, :] = zeros
        vbuf_ref[:, 0:BLOCK, :] = zeros

    @pl.when((b_idx == 0) & (s_idx == 0))
    def _first_projection():
        pnext_ref[...] = _dot(_rms(x_ref[0], g_ref[...]).astype(BF16), win_ref[...])

    low_ts = _lane_halves((ts, LANES))
    q_ref[...] = (pnext_ref[:, 0:A_Q] * (A_HEAD_DIM ** -0.5)).astype(BF16)
    xq_ref[...] = (pnext_ref[:, A_Q + 2 * A_KV:IN_A] * (X_HEAD_DIM ** -0.5)).astype(BF16)
    for src, dst in ((pnext_ref[:, A_Q:A_Q + A_KV], kbuf_ref),
                     (pnext_ref[:, A_Q + A_KV:A_Q + 2 * A_KV], vbuf_ref)):
        swapped = pltpu.roll(src, LANES // 2, axis=1)
        dst[0, BLOCK:BLOCK + ts, :] = jnp.where(low_ts, src, 0.0).astype(BF16)
        dst[1, BLOCK:BLOCK + ts, :] = jnp.where(low_ts, 0.0, swapped).astype(BF16)
        dst[2, BLOCK:BLOCK + ts, :] = jnp.where(low_ts, swapped, 0.0).astype(BF16)
        dst[3, BLOCK:BLOCK + ts, :] = jnp.where(low_ts, 0.0, src).astype(BF16)
    xnn_ref[...] = _rms(xnext_ref[0], g_ref[...]).astype(BF16)

    n_groups = (ts // BLOCK) * A_HEADS // A_ITEMS
    stages_total = 5 * n_groups
    issued = [0, 0]

    def after_stage():
        issued[0] += 1
        while issued[1] < (issued[0] * (IN_A // PROJ_COLS)) // stages_total:
            cols = slice(issued[1] * PROJ_COLS, (issued[1] + 1) * PROJ_COLS)
            pnext_ref[:, cols] = _dot(xnn_ref[...], win_ref[:, cols])
            issued[1] += 1

    low_blk = _lane_halves((BLOCK, LANES))
    from_cur = (lax.broadcasted_iota(jnp.int32, (BLOCK, BLOCK), 1)
                <= lax.broadcasted_iota(jnp.int32, (BLOCK, BLOCK), 0))
    first = jnp.where(s_idx == 0, 1, 0)
    tiles_per_kv = A_GROUP // 2
    all_items = [(i, j, half) for i in range(ts // BLOCK) for j in range(A_HEADS // 2)
                 for half in range(2)]
    band = lambda i: slice(i * BLOCK, (i + 2) * BLOCK)
    kv_slot = lambda j, half: 2 * (j // tiles_per_kv) + half
    for g in range(0, len(all_items), A_ITEMS):
        items = all_items[g:g + A_ITEMS]
        sinks = [sink_ref[0, 2 * j + half] for i, j, half in items]
        band_sc = [_dot_nt(q_ref[i * BLOCK:(i + 1) * BLOCK, j * LANES:(j + 1) * LANES],
                           kbuf_ref[kv_slot(j, half), band(i), :]) for i, j, half in items]
        sc = [jnp.where(from_cur, b[:, BLOCK:2 * BLOCK], b[:, 0:BLOCK])
              + bias_ref[first if i == 0 else 0, 2 * j + half]
              for b, (i, j, half) in zip(band_sc, items)]
        after_stage()
        m = [jnp.maximum(jnp.max(s, axis=-1, keepdims=True), sink) for s, sink in zip(sc, sinks)]
        after_stage()
        p = [jnp.exp(s - mx) for s, mx in zip(sc, m)]
        after_stage()
        den = [jnp.sum(pr, axis=-1, keepdims=True) + jnp.exp(sink - mx)
               for pr, sink, mx in zip(p, sinks, m)]
        after_stage()
        pv = [_dot(jnp.concatenate([jnp.where(from_cur, 0.0, pr), jnp.where(from_cur, pr, 0.0)],
                                   axis=1).astype(BF16),
                   vbuf_ref[kv_slot(j, half), band(i), :])
              for pr, (i, j, half) in zip(p, items)]
        after_stage()
        recip = [1.0 / d for d in den]
        for n in range(0, len(items), 2):
            i, j, _ = items[n]
            o = (pv[n] + pv[n + 1]) * jnp.where(low_blk, recip[n], recip[n + 1])
            cat_ref[i * BLOCK:(i + 1) * BLOCK, j * LANES:(j + 1) * LANES] = o.astype(BF16)

    kbuf_ref[:, 0:BLOCK, :] = kbuf_ref[:, ts:ts + BLOCK, :]
    vbuf_ref[:, 0:BLOCK, :] = vbuf_ref[:, ts:ts + BLOCK, :]

    _cross_attention(xq_ref, mkk_ref, mvv_ref, cat_ref, A_Q, slice(0, ts))
    o_ref[0] = x_ref[0] + _dot(cat_ref[...], wout_ref[0])


def _const_spec(shape):
    n = len(shape)
    return pl.BlockSpec(shape, lambda b, s: (0,) * n)


def _mixer_a(h, layer, rel_bias, sinks, norm_g, w_in, w_out, mkk, mvv, w_gate_up, w_down):
    batch, seq, _ = h.shape
    ts = A_TILE
    assert IN_A % PROJ_COLS == 0
    n_steps = batch * (seq // ts)
    depth = w_gate_up.shape[0]
    gu_slabs = n_steps // depth
    gu_rows = D_MODEL // gu_slabs
    wd_rows = 2 * SUBLANES * 11
    wd_slabs = D_FF // wd_rows
    assert gu_rows % (2 * SUBLANES) == 0 and D_MODEL % gu_slabs == 0 and D_FF % wd_rows == 0
    assert depth * wd_slabs <= n_steps
    step = lambda b, s: b * (seq // ts) + s
    gu_map = lambda b, s: (step(b, s) // gu_slabs, step(b, s) % gu_slabs, 0)
    wd_step = lambda b, s: jnp.minimum(step(b, s), depth * wd_slabs - 1)
    wd_map = lambda b, s: (wd_step(b, s) // wd_slabs, wd_step(b, s) % wd_slabs, 0)
    bucket = jnp.asarray(_t5_bucket_table())
    mem_spec = pl.BlockSpec((1, 1, X_HEADS // 2, 2 * MEM_LEN, LANES), lambda b, s: (layer, b, 0, 0, 0))
    return pl.pallas_call(
        _mixer_a_kernel,
        grid=(batch, seq // ts),
        in_specs=[
            pl.BlockSpec(memory_space=pltpu.SMEM),
            pl.BlockSpec(memory_space=pltpu.SMEM),
            pl.BlockSpec((1, ts, D_MODEL), lambda b, s: (b, s, 0)),
            pl.BlockSpec((1, ts, D_MODEL), _next_tile_map(batch, seq // ts)),
            _const_spec((1, D_MODEL)),
            _const_spec((D_MODEL, IN_A)),
            pl.BlockSpec((1, A_Q + X_Q, D_MODEL), lambda b, s: (layer, 0, 0)),
            mem_spec, mem_spec,
            _const_spec((BLOCK, BLOCK)),
            pl.BlockSpec((1, gu_rows, 2 * D_FF), gu_map),
            pl.BlockSpec((1, wd_rows, D_MODEL), wd_map),
        ],
        out_specs=[
            pl.BlockSpec((1, ts, D_MODEL), lambda b, s: (b, s, 0)),
            pl.BlockSpec((1, gu_rows, 2 * D_FF), gu_map),
            pl.BlockSpec((1, wd_rows, D_MODEL), wd_map),
        ],
        out_shape=[
            jax.ShapeDtypeStruct(h.shape, F32),
            jax.ShapeDtypeStruct(w_gate_up.shape, BF16),
            jax.ShapeDtypeStruct(w_down.shape, BF16),
        ],
        scratch_shapes=[
            pltpu.VMEM((2, A_HEADS, BLOCK, BLOCK), F32),
            pltpu.VMEM((ts, IN_A), F32),
            pltpu.VMEM((ts, D_MODEL), BF16),
            pltpu.VMEM((ts, A_Q), BF16),
            pltpu.VMEM((2 * A_KV_HEADS, ts + BLOCK, LANES), BF16),
            pltpu.VMEM((2 * A_KV_HEADS, ts + BLOCK, LANES), BF16),
            pltpu.VMEM((ts, X_Q), BF16),
            pltpu.VMEM((ts, A_Q + X_Q), BF16),
        ],
        compiler_params=pltpu.CompilerParams(dimension_semantics=("arbitrary", "arbitrary"),
                                             vmem_limit_bytes=VMEM_LIMIT),
    )(rel_bias, sinks.reshape(1, A_HEADS), h, h, norm_g.reshape(1, D_MODEL), w_in.astype(BF16),
      w_out, mkk, mvv, bucket, w_gate_up, w_down)


def _ffn_kernel(x_ref, xnext_ref, g_ref, wgu_ref, cw_ref, cb_ref, wd_ref, fg_ref, o_ref,
                xn_ref, xnn_ref, gbuf_ref, ubuf_ref, tail_ref, acc_ref, *, final_norm):
    ts = x_ref.shape[1]
    n_chunks = D_FF // FF_CHUNK
    cols_of = lambda c: slice(c * FF_CHUNK, (c + 1) * FF_CHUNK)
    slot_of = lambda c: 0 if c == 0 else 1 + c % 2

    def project(c, src_ref):
        xn = src_ref[...]
        gbuf_ref[slot_of(c), SUBLANES:SUBLANES + ts, :] = _dot(xn, wgu_ref[0, :, cols_of(c)])
        ubuf_ref[slot_of(c)] = _dot(xn, wgu_ref[0, :, D_FF + c * FF_CHUNK:D_FF + (c + 1) * FF_CHUNK])

    @pl.when(pl.program_id(1) == 0)
    def _reset_tail():
        tail_ref[...] = jnp.zeros(tail_ref.shape, F32)

    @pl.when((pl.program_id(0) == 0) & (pl.program_id(1) == 0))
    def _first_tile():
        xnn_ref[...] = _rms(x_ref[0], g_ref[...]).astype(BF16)
        project(0, xnn_ref)

    xn_ref[...] = xnn_ref[...]
    acc_ref[...] = x_ref[0]
    for c in range(n_chunks):
        slot = slot_of(c)
        gbuf_ref[slot, 0:SUBLANES, :] = tail_ref[:, cols_of(c)]
        cw = cw_ref[:, cols_of(c)]
        y = (cw[2:3, :] * gbuf_ref[slot, SUBLANES:SUBLANES + ts, :]
             + cw[1:2, :] * gbuf_ref[slot, SUBLANES - 1:SUBLANES - 1 + ts, :]
             + cw[0:1, :] * gbuf_ref[slot, SUBLANES - 2:SUBLANES - 2 + ts, :]
             + cb_ref[:, cols_of(c)])
        tail_ref[:, cols_of(c)] = gbuf_ref[slot, ts:ts + SUBLANES, :]
        act = (_silu(y) * ubuf_ref[slot]).astype(BF16)
        if c + 1 < n_chunks:
            project(c + 1, xn_ref)
        else:
            project(0, xnn_ref)
        acc_ref[...] += _dot(act, wd_ref[0, cols_of(c), :])
        if c == n_chunks // 2:
            xnn_ref[...] = _rms(xnext_ref[0], g_ref[...]).astype(BF16)
    out = acc_ref[...]
    if final_norm:
        out = _rms(out, fg_ref[...])
    o_ref[0] = out


def _ffn(h, layer, norm_g, w_gate_up, conv_w, conv_b, w_down, final_g, final_norm):
    batch, seq, _ = h.shape
    ts = FFN_TILE
    return pl.pallas_call(
        functools.partial(_ffn_kernel, final_norm=final_norm),
        grid=(batch, seq // ts),
        in_specs=[
            pl.BlockSpec((1, ts, D_MODEL), lambda b, s: (b, s, 0)),
            pl.BlockSpec((1, ts, D_MODEL), _next_tile_map(batch, seq // ts)),
            _const_spec((1, D_MODEL)),
            pl.BlockSpec((1, D_MODEL, 2 * D_FF), lambda b, s: (layer, 0, 0)),
            _const_spec((FFN_CONV, D_FF)),
            _const_spec((1, D_FF)),
            pl.BlockSpec((1, D_FF, D_MODEL), lambda b, s: (layer, 0, 0)),
            _const_spec((1, D_MODEL)),
        ],
        out_specs=pl.BlockSpec((1, ts, D_MODEL), lambda b, s: (b, s, 0)),
        out_shape=jax.ShapeDtypeStruct(h.shape, F32),
        scratch_shapes=[
            pltpu.VMEM((ts, D_MODEL), BF16),
            pltpu.VMEM((ts, D_MODEL), BF16),
            pltpu.VMEM((3, ts + SUBLANES, FF_CHUNK), F32),
            pltpu.VMEM((3, ts, FF_CHUNK), F32),
            pltpu.VMEM((SUBLANES, D_FF), F32),
            pltpu.VMEM((ts, D_MODEL), F32),
        ],
        compiler_params=pltpu.CompilerParams(dimension_semantics=("arbitrary", "arbitrary"),
                                             vmem_limit_bytes=VMEM_LIMIT),
    )(h, h, norm_g.reshape(1, D_MODEL), w_gate_up, conv_w, conv_b.reshape(1, D_FF), w_down,
      final_g.reshape(1, D_MODEL))


def _stack_diag(top, bottom):
    zero = jnp.zeros_like(top)
    return jnp.concatenate([jnp.concatenate([top, zero], axis=1),
                            jnp.concatenate([zero, bottom], axis=1)], axis=0)


def _pair_blocks(m, low):
    zero = jnp.zeros_like(m)
    return jnp.concatenate([jnp.where(low, m, zero), jnp.where(low, zero, m)], axis=0)


def _split_bf16(a):
    hi = a.astype(BF16)
    return hi, (a - hi.astype(F32)).astype(BF16)


def _pair_product(l, r, low):
    lh, ll = _split_bf16(l)
    rh, rl = _split_bf16(r)
    rbh, rbl = _pair_blocks(rh, low), _pair_blocks(rl, low)
    rows = l.shape[0]
    both = _dot(jnp.concatenate([lh, ll], axis=0), rbh)
    return both[0:rows, :] + (_dot(lh, rbl) + both[rows:2 * rows, :])


def _pair_mm(ls, rs, low, split=False):
    if split:
        return [_pair_product(l, r, low) for l, r in zip(ls, rs)]
    return [_dot(l.astype(BF16), _pair_blocks(r.astype(BF16), low)) for l, r in zip(ls, rs)]


def _pair_mm_stacked(tops, bottoms, rs, low):
    out = [_pair_product(jnp.concatenate([t, b], axis=0), r, low)
           for t, b, r in zip(tops, bottoms, rs)]
    return [x[0:CHUNK, :] for x in out], [x[CHUNK:2 * CHUNK, :] for x in out]


def _pair_unit_lower_inverses(a, eye, diag_blocks, low):
    d = [jnp.where(diag_blocks, x, 0.0) for x in a]
    e = [x - y for x, y in zip(a, d)]
    x0 = [eye - x for x in d]
    d2 = _pair_mm(d, d, low, split=True)
    t, d4 = _pair_mm_stacked(x0, d2, d2, low)
    x1 = [p + q for p, q in zip(x0, t)]
    t, d8 = _pair_mm_stacked(x1, d4, d4, low)
    x2 = [p + q for p, q in zip(x1, t)]
    td = [p + q for p, q in zip(x2, _pair_mm(x2, d8, low, split=True))]
    n = _pair_mm(td, e, low)
    n2 = _pair_mm(n, n, low)
    v = [p - q for p, q in zip(td, _pair_mm(n, td, low))]
    return [p + q for p, q in zip(v, _pair_mm(n2, v, low))]


def _mixer_b_kernel(x_ref, g_ref, win_ref, cw_ref, alog_ref, dtb_ref, ong_ref, wout_ref,
                    mkk_ref, mvv_ref, o_ref,
                    cbuf_ref, q_ref, k_ref, v_ref, qd_ref, ktb_ref, z_ref, gcol_ref, grow_ref,
                    brow_ref, egrow_ref, dc_ref, xq_ref, osc_ref, state_ref, cat_ref):
    ts = x_ref.shape[1]
    n_pairs = B_V_HEADS // 2
    per_block = LANES // CHUNK
    sub = CHUNK_GROUP * CHUNK
    n_sub = ts // sub

    @pl.when(pl.program_id(1) == 0)
    def _reset_carry():
        cbuf_ref[ts:ts + SUBLANES, :] = jnp.zeros((SUBLANES, B_QKV), F32)
        state_ref[...] = jnp.zeros(state_ref.shape, F32)

    r_t = lax.broadcasted_iota(jnp.int32, (LANES, LANES), 0)
    c_t = lax.broadcasted_iota(jnp.int32, (LANES, LANES), 1)
    same_chunk_tri = jnp.where(r_t >= c_t, r_t // CHUNK, -1) == (c_t // CHUNK)
    tri = jnp.where(same_chunk_tri, 1.0, 0.0).astype(BF16)
    low_sub = _lane_halves((sub, LANES))
    low8 = _lane_halves((SUBLANES, LANES))
    low_row = _lane_halves((1, LANES))
    cw = cw_ref[...]
    cbuf_ref[0:SUBLANES, :] = cbuf_ref[ts:ts + SUBLANES, :]

    def prepare(r):
        rows = slice(r * sub, (r + 1) * sub)
        xn = _rms(x_ref[0, rows, :], g_ref[...]).astype(BF16)
        base = SUBLANES + r * sub
        ba0 = B_QKV + B_V + X_Q
        gates = _dot(xn, win_ref[:, ba0:ba0 + 2 * LANES])
        cbuf_ref[base:base + sub, :] = _dot(xn, win_ref[:, 0:B_QKV])
        z_ref[rows, :] = _dot(xn, win_ref[:, B_QKV:B_QKV + B_V])
        xq_ref[rows, :] = _dot(xn, win_ref[:, B_QKV + B_V:ba0]).astype(BF16)
        beta = 1.0 / (1.0 + jnp.exp(-gates[:, 0:LANES]))
        a_logit = gates[:, LANES:2 * LANES] + dtb_ref[...]
        softplus = jnp.maximum(a_logit, 0.0) + jnp.log1p(jnp.exp(-jnp.abs(a_logit)))
        gl = -jnp.exp(alog_ref[...]) * softplus

        g_hi = gl.astype(BF16)
        g_rest = gl - g_hi.astype(F32)
        g_mid = g_rest.astype(BF16)
        g_lo = (g_rest - g_mid.astype(F32)).astype(BF16)
        gc = jnp.concatenate(
            [_dot(tri, g_hi[blk]) + (_dot(tri, g_mid[blk]) + _dot(tri, g_lo[blk]))
             for blk in (slice(i * LANES, (i + 1) * LANES) for i in range(sub // LANES))],
            axis=0)
        eg = jnp.exp(gc)

        def conv_silu(c0):
            cols = slice(c0, c0 + LANES)
            y = cw[B_CONV - 1:B_CONV, cols] * cbuf_ref[base:base + sub, cols]
            for tap in range(1, B_CONV):
                y = y + (cw[B_CONV - 1 - tap:B_CONV - tap, cols]
                         * cbuf_ref[base - tap:base - tap + sub, cols])
            return _silu(y)

        k_heads = []
        for hh in range(B_QK_HEADS):
            cols = slice(hh * LANES, (hh + 1) * LANES)
            qh = conv_silu(hh * LANES)
            kh = conv_silu(B_QK + hh * LANES)
            qh = qh * lax.rsqrt(jnp.sum(qh * qh, axis=-1, keepdims=True) + EPS) * (B_HEAD_DIM ** -0.5)
            kh = kh * lax.rsqrt(jnp.sum(kh * kh, axis=-1, keepdims=True) + EPS)
            q_ref[rows, cols] = qh.astype(BF16)
            k_ref[rows, cols] = kh.astype(BF16)
            k_heads.append(kh)
            for h in (2 * hh, 2 * hh + 1):
                qd_ref[rows, h * LANES:(h + 1) * LANES] = (qh * eg[:, h:h + 1]).astype(BF16)
        for h in range(B_V_HEADS):
            v_ref[rows, h * LANES:(h + 1) * LANES] = conv_silu(2 * B_QK + h * LANES).astype(BF16)
        for p in range(n_pairs):
            gcol_ref[p, rows, :] = jnp.where(low_sub, gc[:, 2 * p:2 * p + 1],
                                             gc[:, 2 * p + 1:2 * p + 2])

        for blk in range(sub // LANES):
            loc = slice(blk * LANES, (blk + 1) * LANES)
            tok = slice(r * sub + blk * LANES, r * sub + (blk + 1) * LANES)
            g8 = gc[loc, :].T[0:SUBLANES, :]
            b8 = beta[loc, :].T[0:SUBLANES, :]
            e8 = jnp.exp(g8)
            g_last = [g8[:, (half + 1) * CHUNK - 1:(half + 1) * CHUNK] for half in range(per_block)]
            g_last8 = jnp.where(low8, g_last[0], g_last[1])
            kt8 = jnp.exp(g_last8 - g8) * b8
            swapped = [pltpu.roll(t, CHUNK, axis=1) for t in (g8, b8, e8, kt8)]
            for half in range(per_block):
                c = (r * sub + blk * LANES) // CHUNK + half
                dc_ref[c] = jnp.exp(jnp.broadcast_to(g_last[half], (SUBLANES, LANES)))
                for p in range(n_pairs):
                    pair_rows = [jnp.where(low_row,
                                           (t if half == 0 else t_sw)[2 * p:2 * p + 1, :],
                                           (t_sw if half == 0 else t)[2 * p + 1:2 * p + 2, :])
                                 for t, t_sw in zip((g8, b8, e8, kt8), swapped)]
                    grow_ref[c, p:p + 1, :] = pair_rows[0]
                    brow_ref[c, p:p + 1, :] = pair_rows[1]
                    egrow_ref[c, p:p + 1, :] = pair_rows[2]
                    k_c = k_heads[p][blk * LANES + half * CHUNK:blk * LANES + (half + 1) * CHUNK, :]
                    k_twice_t = jnp.concatenate([k_c, k_c], axis=0).T
                    ktb_ref[c, p] = (k_twice_t * pair_rows[3]).astype(BF16)

    row = lax.broadcasted_iota(jnp.int32, (CHUNK, LANES), 0)
    col = lax.broadcasted_iota(jnp.int32, (CHUNK, LANES), 1) % CHUNK
    low = _lane_halves((CHUNK, LANES))
    incl = row >= col
    strict = row > col
    eye = jnp.where(row == col, 1.0, 0.0)
    diag_blocks = (row // SUB) == (col // SUB)
    rows_of = lambda c: slice(c * CHUNK, (c + 1) * CHUNK)
    cols_of = lambda i: slice(i * LANES, (i + 1) * LANES)

    def delta_rule(r, states):
        chunks = range(r * CHUNK_GROUP, (r + 1) * CHUNK_GROUP)
        def state_free(items):
            k_c = [k_ref[rows_of(c), cols_of(p)] for c, p in items]
            k_twice = [jnp.concatenate([kc, kc], axis=0) for kc in k_c]
            kq = [_dot_nt(jnp.concatenate([kc, q_ref[rows_of(c), cols_of(p)]], axis=0), k2)
                  for (c, p), kc, k2 in zip(items, k_c, k_twice)]
            kk = [x[0:CHUNK, :] for x in kq]
            qk = [x[CHUNK:2 * CHUNK, :] for x in kq]
            decay = [jnp.exp(jnp.where(incl, gcol_ref[p, rows_of(c), :] - grow_ref[c, p:p + 1, :],
                                       NEG_INF)) for c, p in items]
            b_row = [brow_ref[c, p:p + 1, :] for c, p in items]
            a = [kk_i * jnp.where(strict, d_i, 0.0) * b_i for kk_i, d_i, b_i in zip(kk, decay, b_row)]
            attn = [(qk_i * d_i * b_i).astype(BF16) for qk_i, d_i, b_i in zip(qk, decay, b_row)]
            t_inv = _pair_unit_lower_inverses(a, eye, diag_blocks, low)
            u = [_dot(t_i.astype(BF16), _stack_diag(v_ref[rows_of(c), cols_of(2 * p)],
                                                    v_ref[rows_of(c), cols_of(2 * p + 1)]))
                 for t_i, (c, p) in zip(t_inv, items)]
            w = [_dot((t_i * egrow_ref[c, p:p + 1, :]).astype(BF16), _stack_diag(kc, kc)).astype(BF16)
                 for t_i, (c, p), kc in zip(t_inv, items, k_c)]
            return attn, u, w

        chunks_per_group = PHASE_ITEMS // n_pairs
        for g0 in range(0, CHUNK_GROUP, chunks_per_group):
            group = chunks[g0:g0 + chunks_per_group]
            attn, u, w = state_free([(c, p) for c in group for p in range(n_pairs)])
            for ci, c in enumerate(group):
                pairs = range(n_pairs)
                w_s = [_dot(jnp.concatenate([w[ci * n_pairs + p],
                                             qd_ref[rows_of(c), 2 * p * LANES:(2 * p + 2) * LANES]],
                                            axis=0),
                            _stack_diag(states[2 * p].astype(BF16), states[2 * p + 1].astype(BF16)))
                       for p in pairs]
                delta = [(u[ci * n_pairs + p] - w_s[p][0:CHUNK, :]).astype(BF16) for p in pairs]
                upd = [_dot(jnp.concatenate([ktb_ref[c, p], attn[ci * n_pairs + p]], axis=0),
                            _stack_diag(delta[p][:, 0:LANES], delta[p][:, LANES:2 * LANES]))
                       for p in pairs]
                for p in pairs:
                    for i, h in enumerate((2 * p, 2 * p + 1)):
                        states[h] = (dc_ref[c, h:h + 1, :] * states[h]
                                     + upd[p][0:B_HEAD_DIM, cols_of(i)])
                    osc_ref[rows_of(c), 2 * p * LANES:(2 * p + 2) * LANES] = (
                        w_s[p][CHUNK:, :] + upd[p][B_HEAD_DIM:, :])
        return states

    def finish(r):
        rows = slice(r * sub, (r + 1) * sub)
        ong = ong_ref[...]
        for h in range(B_V_HEADS):
            o = osc_ref[rows, cols_of(h)]
            o = o * lax.rsqrt(jnp.mean(o * o, axis=-1, keepdims=True) + EPS) * ong
            cat_ref[rows, cols_of(h)] = (o * _silu(z_ref[rows, cols_of(h)])).astype(BF16)
        _cross_attention(xq_ref, mkk_ref, mvv_ref, cat_ref, B_V, rows)
        o_ref[0, rows, :] = x_ref[0, rows, :] + _dot(cat_ref[rows, :], wout_ref[0])

    for r in range(n_sub):
        prepare(r)
    states = [state_ref[h] for h in range(B_V_HEADS)]
    for r in range(n_sub):
        states = delta_rule(r, states)
    for h in range(B_V_HEADS):
        state_ref[h] = states[h]
    for r in range(n_sub):
        finish(r)


def _mixer_b(h, layer, norm_g, w_in, conv_w, a_log, dt_bias, out_norm_g, w_out, mkk, mvv):
    batch, seq, _ = h.shape
    ts = B_TILE
    c0 = B_QKV + B_V
    w_bf = w_in.astype(BF16)
    pad = jnp.zeros((D_MODEL, LANES - B_V_HEADS), BF16)
    w_in_k = jnp.concatenate([
        w_bf[:, 0:c0],
        w_bf[:, IN_B - X_Q:] * (X_HEAD_DIM ** -0.5),
        w_bf[:, c0:c0 + B_V_HEADS], pad,
        w_bf[:, c0 + B_V_HEADS:c0 + 2 * B_V_HEADS], pad], axis=1)
    lane_pad = lambda t: jnp.pad(t.reshape(1, B_V_HEADS), ((0, 0), (0, LANES - B_V_HEADS)))
    mem_spec = pl.BlockSpec((1, 1, X_HEADS // 2, 2 * MEM_LEN, LANES), lambda b, s: (layer, b, 0, 0, 0))
    n_chunks = ts // CHUNK
    return pl.pallas_call(
        _mixer_b_kernel,
        grid=(batch, seq // ts),
        in_specs=[
            pl.BlockSpec((1, ts, D_MODEL), lambda b, s: (b, s, 0)),
            _const_spec((1, D_MODEL)),
            _const_spec((D_MODEL, IN_B_PAD)),
            _const_spec((B_CONV, B_QKV)),
            _const_spec((1, LANES)),
            _const_spec((1, LANES)),
            _const_spec((1, B_HEAD_DIM)),
            pl.BlockSpec((1, B_V + X_Q, D_MODEL), lambda b, s: (layer, 0, 0)),
            mem_spec, mem_spec,
        ],
        out_specs=pl.BlockSpec((1, ts, D_MODEL), lambda b, s: (b, s, 0)),
        out_shape=jax.ShapeDtypeStruct(h.shape, F32),
        scratch_shapes=[
            pltpu.VMEM((ts + SUBLANES, B_QKV), F32),
            pltpu.VMEM((ts, B_QK), BF16),
            pltpu.VMEM((ts, B_QK), BF16),
            pltpu.VMEM((ts, B_V), BF16),
            pltpu.VMEM((ts, B_V), BF16),
            pltpu.VMEM((n_chunks, B_V_HEADS // 2, B_HEAD_DIM, LANES), BF16),
            pltpu.VMEM((ts, B_V), F32),
            pltpu.VMEM((B_V_HEADS // 2, ts, LANES), F32),
            pltpu.VMEM((n_chunks, SUBLANES, LANES), F32),
            pltpu.VMEM((n_chunks, SUBLANES, LANES), F32),
            pltpu.VMEM((n_chunks, SUBLANES, LANES), F32),
            pltpu.VMEM((n_chunks, SUBLANES, LANES), F32),
            pltpu.VMEM((ts, X_Q), BF16),
            pltpu.VMEM((ts, B_V), F32),
            pltpu.VMEM((B_V_HEADS, B_HEAD_DIM, B_HEAD_DIM), F32),
            pltpu.VMEM((ts, B_V + X_Q), BF16),
        ],
        compiler_params=pltpu.CompilerParams(dimension_semantics=("arbitrary", "arbitrary"),
                                             vmem_limit_bytes=VMEM_LIMIT),
    )(h, norm_g.reshape(1, D_MODEL), w_in_k, conv_w, lane_pad(a_log), lane_pad(dt_bias),
      out_norm_g.reshape(1, B_HEAD_DIM), w_out, mkk, mvv)


def kernel(x, mem, rel_bias, norm_mix_g, norm_mem_g, w_mem_kv, w_out, w_in_a, sinks_a, w_in_b,
           conv_qkv_b, a_log_b, dt_bias_b, out_norm_g_b, norm_ffn_g, w_gate_up, ffn_conv_w,
           ffn_conv_b, w_down, final_norm_g):
    mkk, mvv = _mem_kv(mem, norm_mem_g, w_mem_kv)
    w_out_bf = w_out.astype(BF16)
    h, w_gate_up_bf, w_down_bf = _mixer_a(x, 0, rel_bias, sinks_a[0], norm_mix_g[0], w_in_a[0],
                                          w_out_bf, mkk, mvv, w_gate_up, w_down)
    h = _ffn(h, 0, norm_ffn_g[0], w_gate_up_bf, ffn_conv_w[0], ffn_conv_b[0], w_down_bf,
             final_norm_g, False)
    h = _mixer_b(h, 1, norm_mix_g[1], w_in_b[0], conv_qkv_b[0], a_log_b[0], dt_bias_b[0],
                 out_norm_g_b[0], w_out_bf, mkk, mvv)
    h = _ffn(h, 1, norm_ffn_g[1], w_gate_up_bf, ffn_conv_w[1], ffn_conv_b[1], w_down_bf,
             final_norm_g, True)
    return h
```
